```python
import math
import jax
import jax.numpy as jnp
from jax import lax
import numpy as np

D_MODEL = 1024
BATCH = 8
SEQ = 2048
DEPTH = 2

MEM_LEN = 256
HEAD_DIM = 64
SWA_Q_HEADS = 8
SWA_KV_HEADS = 2
SWA_GROUP = SWA_Q_HEADS // SWA_KV_HEADS
WINDOW = 128
SWA_BLOCK = WINDOW
MLSTM_HEADS = 8
MLSTM_HEAD_DIM = 64
MLSTM_CHUNK = 64
CONV_WIDTH = 4
X_HEADS = 4
X_HEAD_DIM = 128
N_BRANCH = 3
D_FF = 4 * D_MODEL
REL_BUCKETS = 32
REL_MAX_EXACT = 16
REL_MAX_DIST = 128
EPS = 1e-6
NEG_INF = -1e30

SWA_Q = SWA_Q_HEADS * HEAD_DIM
SWA_KV = SWA_KV_HEADS * HEAD_DIM
MLSTM_W = MLSTM_HEADS * MLSTM_HEAD_DIM
X_W = X_HEADS * X_HEAD_DIM
IN_SIZES = (SWA_Q, SWA_KV, SWA_KV, MLSTM_W, MLSTM_W, MLSTM_W, MLSTM_HEADS, MLSTM_HEADS, MLSTM_W, X_W, N_BRANCH * D_MODEL)
IN_COLS = sum(IN_SIZES)

kernel_name = "hybrid_swa_mlstm_xattn_gated_block"


def rms_norm(x, g):
    xf = x.astype(jnp.float32)
    y = xf * lax.rsqrt(jnp.mean(xf * xf, axis=-1, keepdims=True) + EPS)
    return (y * g.astype(jnp.float32)).astype(x.dtype)


def t5_causal_bucket(dist):
    n = jnp.maximum(dist, 0)
    nf = jnp.maximum(n, 1).astype(jnp.float32)
    scale = (REL_BUCKETS - REL_MAX_EXACT) / math.log(REL_MAX_DIST / REL_MAX_EXACT)
    large = REL_MAX_EXACT + (jnp.log(nf / REL_MAX_EXACT) * scale).astype(jnp.int32)
    large = jnp.minimum(large, REL_BUCKETS - 1)
    return jnp.where(n < REL_MAX_EXACT, n, large)


def swa_band_bias_and_mask(rel_bias, n_blocks):
    qi = jnp.arange(SWA_BLOCK)[:, None]
    kj = jnp.arange(2 * SWA_BLOCK)[None, :]
    dist = qi + SWA_BLOCK - kj
    band = (dist >= 0) & (dist < WINDOW)
    key_pos = (jnp.arange(n_blocks)[:, None, None] - 1) * SWA_BLOCK + kj[None]
    mask = band[None] & (key_pos >= 0)
    bias = rel_bias.astype(jnp.float32)[t5_causal_bucket(dist)]
    bias = jnp.transpose(bias, (2, 0, 1)).reshape(SWA_KV_HEADS, SWA_GROUP, SWA_BLOCK, 2 * SWA_BLOCK)
    return bias, mask


def swa_attention(q, k, v, sinks, bias, mask):
    B, T, _ = q.shape
    nb = T // SWA_BLOCK
    q = q.reshape(B, nb, SWA_BLOCK, SWA_KV_HEADS, SWA_GROUP, HEAD_DIM)
    k = k.reshape(B, nb, SWA_BLOCK, SWA_KV_HEADS, HEAD_DIM)
    v = v.reshape(B, nb, SWA_BLOCK, SWA_KV_HEADS, HEAD_DIM)

    def with_prev(t):
        prev = jnp.pad(t[:, :-1], ((0, 0), (1, 0), (0, 0), (0, 0), (0, 0)))
        return jnp.concatenate([prev, t], axis=2)

    kb, vb = with_prev(k), with_prev(v)
    s = jnp.einsum("bnqhgd,bnkhd->bnhgqk", q, kb).astype(jnp.float32) * HEAD_DIM ** -0.5
    s = s + bias[None, None]
    s = jnp.where(mask[None, :, None, None], s, NEG_INF)
    sink = sinks.astype(jnp.float32).reshape(SWA_KV_HEADS, SWA_GROUP)[:, :, None, None]
    m = jnp.maximum(jnp.max(s, axis=-1, keepdims=True), sink)
    p = jnp.exp(s - m)
    p = p / (jnp.sum(p, axis=-1, keepdims=True) + jnp.exp(sink - m))
    o = jnp.einsum("bnhgqk,bnkhd->bnqhgd", p.astype(v.dtype), vb)
    return o.reshape(B, T, SWA_Q)


def causal_depthwise_conv(x, w):
    C = x.shape[-1]
    return lax.conv_general_dilated(
        x, w[:, None, :].astype(x.dtype), window_strides=(1,),
        padding=[(CONV_WIDTH - 1, 0)], dimension_numbers=("NWC", "WIO", "NWC"),
        feature_group_count=C)


def mlstm_chunkwise(q, k, v, i_pre, f_pre):
    B, T, H, Dh = q.shape
    L = MLSTM_CHUNK
    nc = T // L
    f32 = jnp.float32
    qc = q.astype(f32).reshape(B, nc, L, H, Dh)
    kc = (k.astype(f32) * Dh ** -0.5).reshape(B, nc, L, H, Dh)
    vc = v.astype(f32).reshape(B, nc, L, H, Dh)
    ig = i_pre.reshape(B, nc, L, H)
    b = jnp.cumsum(jax.nn.log_sigmoid(f_pre).reshape(B, nc, L, H), axis=2)
    g = b[:, :, -1]
    w_log = g[:, :, None] - b + ig
    m_loc = jnp.max(w_log, axis=2)
    wk = jnp.exp(w_log - m_loc[:, :, None])[..., None] * kc
    dC = jnp.einsum("bclhk,bclhv->bchkv", wk, vc)
    dn = jnp.sum(wk, axis=2)

    def step(carry, xs):
        C, n, m = carry
        dC_c, dn_c, g_c, ml_c = xs
        m_new = jnp.maximum(g_c + m, ml_c)
        a = jnp.exp(g_c + m - m_new)
        s = jnp.exp(ml_c - m_new)
        C_new = a[..., None, None] * C + s[..., None, None] * dC_c
        n_new = a[..., None] * n + s[..., None] * dn_c
        return (C_new, n_new, m_new), (C, n, m)

    init = (jnp.zeros((B, H, Dh, Dh), f32), jnp.zeros((B, H, Dh), f32), jnp.zeros((B, H), f32))
    xs = tuple(jnp.moveaxis(t, 1, 0) for t in (dC, dn, g, m_loc))
    _, (C_in, n_in, m_in) = lax.scan(step, init, xs)
    C_in = jnp.moveaxis(C_in, 0, 1)
    n_in = jnp.moveaxis(n_in, 0, 1)
    m_in = jnp.moveaxis(m_in, 0, 1)

    causal = jnp.tril(jnp.ones((L, L), dtype=bool))
    d_log = b[:, :, :, None] - b[:, :, None, :] + ig[:, :, None, :]
    d_log = jnp.where(causal[:, :, None], d_log, NEG_INF)
    inter_log = b + m_in[:, :, None]
    m_t = jnp.maximum(jnp.max(d_log, axis=3), inter_log)
    s = jnp.einsum("bcthd,bcshd->bctsh", qc, kc) * jnp.exp(d_log - m_t[:, :, :, None])
    inter_w = jnp.exp(inter_log - m_t)
    num = jnp.einsum("bctsh,bcshd->bcthd", s, vc) + inter_w[..., None] * jnp.einsum("bcthk,bchkv->bcthv", qc, C_in)
    den = jnp.sum(s, axis=3) + inter_w * jnp.einsum("bcthk,bchk->bcth", qc, n_in)
    h = num / jnp.maximum(jnp.abs(den), jnp.exp(-m_t))[..., None]
    return h.reshape(B, T, H, Dh)


def mlstm_branch(mq, mk, mv, mi, mf, mo, conv_w, b_i, b_f, norm_g):
    B, T, _ = mq.shape
    qk = jax.nn.silu(causal_depthwise_conv(jnp.concatenate([mq, mk], axis=-1), conv_w))
    q, k = jnp.split(qk, 2, axis=-1)
    shp = (B, T, MLSTM_HEADS, MLSTM_HEAD_DIM)
    h = mlstm_chunkwise(q.reshape(shp), k.reshape(shp), mv.reshape(shp),
                        (mi + b_i).astype(jnp.float32), (mf + b_f).astype(jnp.float32))
    mu = jnp.mean(h, axis=-1, keepdims=True)
    var = jnp.mean(jnp.square(h - mu), axis=-1, keepdims=True)
    h = ((h - mu) * lax.rsqrt(var + EPS)).reshape(B, T, MLSTM_W) * norm_g.astype(jnp.float32)
    return (jax.nn.sigmoid(mo.astype(jnp.float32)) * h).astype(mq.dtype)


def cross_attention(q, k, v):
    B, T, _ = q.shape
    M = k.shape[1]
    q = q.reshape(B, T, X_HEADS, X_HEAD_DIM)
    k = k.reshape(B, M, X_HEADS, X_HEAD_DIM)
    v = v.reshape(B, M, X_HEADS, X_HEAD_DIM)
    s = jnp.einsum("bthd,bmhd->bhtm", q, k).astype(jnp.float32) * X_HEAD_DIM ** -0.5
    p = jax.nn.softmax(s, axis=-1).astype(v.dtype)
    return jnp.einsum("bhtm,bmhd->bthd", p, v).reshape(B, T, X_W)


def setup_inputs(seed: int = 0) -> dict:
    key = jax.random.key(seed)
    ks = jax.random.split(key, 20)
    f32 = jnp.float32

    def nrm(k, shape, scale):
        return jax.random.normal(k, shape, f32) * scale

    def gain(k, shape):
        return 1.0 + 0.05 * jax.random.normal(k, shape, f32)

    f_bias = jnp.linspace(3.0, 6.0, MLSTM_HEADS, dtype=f32)[None] + 0.1 * jax.random.normal(ks[8], (DEPTH, MLSTM_HEADS), f32)
    return {
        "x": nrm(ks[0], (BATCH, SEQ, D_MODEL), 1.0),
        "mem": nrm(ks[1], (BATCH, MEM_LEN, D_MODEL), 1.0),
        "rel_bias": nrm(ks[2], (REL_BUCKETS, SWA_Q_HEADS), 0.5),
        "g_mix": gain(ks[3], (DEPTH, D_MODEL)),
        "w_in": nrm(ks[4], (DEPTH, D_MODEL, IN_COLS), D_MODEL ** -0.5),
        "conv_w": nrm(ks[5], (DEPTH, CONV_WIDTH, 2 * MLSTM_W), CONV_WIDTH ** -0.5),
        "b_i": nrm(ks[6], (DEPTH, MLSTM_HEADS), 0.1),
        "b_f": f_bias,
        "mlstm_norm_g": gain(ks[7], (DEPTH, MLSTM_W)),
        "sinks": nrm(ks[9], (DEPTH, SWA_Q_HEADS), 0.5),
        "g_mem": gain(ks[10], (DEPTH, D_MODEL)),
        "w_mem_kv": nrm(ks[11], (DEPTH, D_MODEL, 2 * X_W), D_MODEL ** -0.5),
        "w_br_swa": nrm(ks[12], (DEPTH, SWA_Q, D_MODEL), SWA_Q ** -0.5),
        "w_br_mlstm": nrm(ks[13], (DEPTH, MLSTM_W, D_MODEL), MLSTM_W ** -0.5),
        "w_br_x": nrm(ks[14], (DEPTH, X_W, D_MODEL), X_W ** -0.5),
        "w_out": nrm(ks[15], (DEPTH, D_MODEL, D_MODEL), D_MODEL ** -0.5),
        "g_ffn": gain(ks[16], (DEPTH, D_MODEL)),
        "w_ff1": nrm(ks[17], (DEPTH, D_MODEL, D_FF), D_MODEL ** -0.5),
        "w_ff2": nrm(ks[18], (DEPTH, D_FF, D_MODEL), D_FF ** -0.5),
        "g_final": gain(ks[19], (D_MODEL,)),
    }


def reference(x, mem, rel_bias, g_mix, w_in, conv_w, b_i, b_f, mlstm_norm_g, sinks, g_mem, w_mem_kv,
              w_br_swa, w_br_mlstm, w_br_x, w_out, g_ffn, w_ff1, w_ff2, g_final):
    B, T, _ = x.shape
    n_blocks = T // SWA_BLOCK
    band_bias, band_mask = swa_band_bias_and_mask(rel_bias, n_blocks)
    split_at = [int(c) for c in np.cumsum(IN_SIZES)[:-1]]
    for l in range(DEPTH):
        h = rms_norm(x, g_mix[l])
        proj = h @ w_in[l]
        sq, sk, sv, mq, mk, mv, mi, mf, mo, xq, gate_pre = jnp.split(proj, split_at, axis=-1)
        y_swa = swa_attention(sq, sk, sv, sinks[l], band_bias, band_mask)
        y_mlstm = mlstm_branch(mq, mk, mv, mi, mf, mo, conv_w[l], b_i[l], b_f[l], mlstm_norm_g[l])
        mem_kv = rms_norm(mem, g_mem[l]) @ w_mem_kv[l]
        mk_x, mv_x = jnp.split(mem_kv, 2, axis=-1)
        y_x = cross_attention(xq, mk_x, mv_x)
        gates = jax.nn.sigmoid(gate_pre.astype(jnp.float32)).astype(x.dtype).reshape(B, T, N_BRANCH, D_MODEL)
        merged = (gates[:, :, 0] * (y_swa @ w_br_swa[l])
                  + gates[:, :, 1] * (y_mlstm @ w_br_mlstm[l])
                  + gates[:, :, 2] * (y_x @ w_br_x[l]))
        x = x + merged @ w_out[l]
        h = rms_norm(x, g_ffn[l])
        x = x + jnp.square(jax.nn.relu(h @ w_ff1[l])) @ w_ff2[l]
    return rms_norm(x, g_final)
```

```python
import functools
import math

import numpy as np
import jax
import jax.numpy as jnp
from jax import lax
from jax.experimental import pallas as pl
from jax.experimental.pallas import tpu as pltpu

D_MODEL = 1024
HEAD_DIM = 64
SWA_Q_HEADS = 8
SWA_KV_HEADS = 2
SWA_GROUP = SWA_Q_HEADS // SWA_KV_HEADS
WINDOW = 128
MLSTM_HEADS = 8
MLSTM_HEAD_DIM = 64
CONV_WIDTH = 4
X_HEADS = 4
X_HEAD_DIM = 128
N_BRANCH = 3
D_FF = 4 * D_MODEL
REL_BUCKETS = 32
REL_MAX_EXACT = 16
REL_MAX_DIST = 128
EPS = 1e-6
NEG_INF = -1e30

SWA_Q = SWA_Q_HEADS * HEAD_DIM
SWA_KV = SWA_KV_HEADS * HEAD_DIM
MLSTM_W = MLSTM_HEADS * MLSTM_HEAD_DIM
X_W = X_HEADS * X_HEAD_DIM

C_SWA = 0
C_MQK = C_SWA + SWA_Q + 2 * SWA_KV
C_MV = C_MQK + 2 * MLSTM_W
C_MO = C_MV + MLSTM_W
C_XQ = C_MO + MLSTM_W
C_GATE = C_XQ + X_W
C_END = C_GATE + N_BRANCH * D_MODEL
IF_PAD = 128

MLSTM_CHUNK = 128
CONV_HALO = 8

VMEM_LIMIT = 56 * 1024 * 1024

BF16 = jnp.bfloat16
F32 = jnp.float32


def _rms(x, g):
    return x * lax.rsqrt(jnp.mean(x * x, axis=-1, keepdims=True) + EPS) * g


def _const_spec(shape):
    nd = len(shape)
    return pl.BlockSpec(shape, lambda *_: (0,) * nd, pipeline_mode=pl.Buffered(1))


def _dot(a, b):
    return jnp.dot(a, b, preferred_element_type=F32)


def _dot_nt(a, b):
    return lax.dot_general(a, b, (((1,), (1,)), ((), ())), preferred_element_type=F32)


def _in_proj_kernel(x_ref, g_ref, w_ref, wif_ref, bif_ref, convw_ref,
                    swa_ref, mqk_ref, mv_ref, mo_ref, xq_ref, gate_ref, mif_ref,
                    conv_buf, *, blocks_per_seq):
    i = pl.program_id(0)
    tm = x_ref.shape[0]
    h = _rms(x_ref[...], g_ref[...]).astype(BF16)

    def proj(lo, width):
        return _dot(h, w_ref[:, lo:lo + width])

    swa_ref[...] = proj(C_SWA, SWA_Q + 2 * SWA_KV).astype(BF16)
    mv_ref[...] = proj(C_MV, MLSTM_W).astype(BF16)
    mo_ref[...] = jax.nn.sigmoid(proj(C_MO, MLSTM_W)).astype(BF16)
    xq_ref[...] = proj(C_XQ, X_W).astype(BF16)
    for k in range(N_BRANCH):
        gate_ref[:, k * D_MODEL:(k + 1) * D_MODEL] = jax.nn.sigmoid(
            proj(C_GATE + k * D_MODEL, D_MODEL)).astype(BF16)
    mif_ref[...] = _dot(h, wif_ref[...]) + bif_ref[...]

    @pl.when(i % blocks_per_seq == 0)
    def _():
        conv_buf[0:CONV_HALO, :] = jnp.zeros((CONV_HALO, 2 * MLSTM_W), F32)

    @pl.when(i % blocks_per_seq != 0)
    def _():
        conv_buf[0:CONV_HALO, :] = conv_buf[tm:tm + CONV_HALO, :]

    conv_buf[CONV_HALO:CONV_HALO + tm, :] = proj(C_MQK, 2 * MLSTM_W)
    for part, scale in ((0, 1.0), (1, MLSTM_HEAD_DIM ** -0.5)):
        cols = slice(part * MLSTM_W, (part + 1) * MLSTM_W)
        acc = None
        for j in range(CONV_WIDTH):
            r0 = CONV_HALO - (CONV_WIDTH - 1) + j
            term = convw_ref[j:j + 1, cols] * conv_buf[r0:r0 + tm, cols]
            acc = term if acc is None else acc + term
        act = acc * jax.nn.sigmoid(acc)
        mqk_ref[:, cols] = (act * scale).astype(BF16)


def _in_proj(x2, g, w, wif, bif, convw, seq_len, tm=512):
    n = x2.shape[0]
    assert n % tm == 0 and seq_len % tm == 0
    row = lambda width: pl.BlockSpec((tm, width), lambda i: (i, 0))
    out_widths = (SWA_Q + 2 * SWA_KV, 2 * MLSTM_W, MLSTM_W, MLSTM_W, X_W, N_BRANCH * D_MODEL)
    out_shape = [jax.ShapeDtypeStruct((n, wd), BF16) for wd in out_widths]
    out_shape.append(jax.ShapeDtypeStruct((n, IF_PAD), F32))
    return pl.pallas_call(
        functools.partial(_in_proj_kernel, blocks_per_seq=seq_len // tm),
        grid=(n // tm,),
        in_specs=[row(D_MODEL), _const_spec((1, D_MODEL)), _const_spec((D_MODEL, C_END)),
                  _const_spec((D_MODEL, IF_PAD)), _const_spec((1, IF_PAD)),
                  _const_spec((CONV_WIDTH, 2 * MLSTM_W))],
        out_specs=[row(wd) for wd in out_widths] + [row(IF_PAD)],
        out_shape=out_shape,
        scratch_shapes=[pltpu.VMEM((tm + CONV_HALO, 2 * MLSTM_W), F32)],
        compiler_params=pltpu.CompilerParams(
            dimension_semantics=("arbitrary",), vmem_limit_bytes=VMEM_LIMIT),
        name="in_proj",
    )(x2, g, w, wif, bif, convw)


def _t5_bucket_table():
    qi = np.arange(WINDOW)[:, None]
    kj = np.arange(2 * WINDOW)[None, :]
    dist = qi + WINDOW - kj
    n = np.maximum(dist, 0)
    nf = np.maximum(n, 1).astype(np.float32)
    scale = np.float32((REL_BUCKETS - REL_MAX_EXACT) / math.log(REL_MAX_DIST / REL_MAX_EXACT))
    large = REL_MAX_EXACT + (np.log(nf / np.float32(REL_MAX_EXACT)) * scale).astype(np.int32)
    large = np.minimum(large, REL_BUCKETS - 1)
    return np.where(n < REL_MAX_EXACT, n, large).astype(np.int32)


def _bias_kernel(rel_ref, bucket_ref, o_ref):
    bucket = bucket_ref[...]
    for h in range(SWA_Q_HEADS):
        acc = jnp.zeros(bucket.shape, F32)
        for b in range(REL_BUCKETS):
            acc = jnp.where(bucket == b, rel_ref[b, h], acc)
        o_ref[h] = acc


def _bias_table(rel_bias):
    bucket = jnp.asarray(_t5_bucket_table())
    return pl.pallas_call(
        _bias_kernel,
        in_specs=[pl.BlockSpec(memory_space=pltpu.SMEM),
                  pl.BlockSpec((WINDOW, 2 * WINDOW), lambda: (0, 0))],
        out_specs=pl.BlockSpec((SWA_Q_HEADS, WINDOW, 2 * WINDOW), lambda: (0, 0, 0)),
        out_shape=jax.ShapeDtypeStruct((SWA_Q_HEADS, WINDOW, 2 * WINDOW), F32),
        name="t5_bias_table",
    )(rel_bias.astype(F32), bucket)


def _swa_kernel(sinks_ref, cur_ref, prev_ref, bias_ref, o_ref):
    n = pl.program_id(1)
    tq = cur_ref.shape[0]
    rows = SWA_GROUP * WINDOW
    qi = lax.broadcasted_iota(jnp.int32, (rows, 2 * WINDOW), 0) % WINDOW
    kj = lax.broadcasted_iota(jnp.int32, (rows, 2 * WINDOW), 1)
    dist = qi + WINDOW - kj
    band = (dist >= 0) & (dist < WINDOW)
    kc, vc = SWA_Q, SWA_Q + SWA_KV
    for blk in range(tq // WINDOW):
        r0 = blk * WINDOW
        for hk in range(SWA_KV_HEADS):
            hc = slice(hk * HEAD_DIM, (hk + 1) * HEAD_DIM)
            if blk == 0:
                k2 = jnp.concatenate([prev_ref[:, hc], cur_ref[0:WINDOW, kc + hk * HEAD_DIM:kc + (hk + 1) * HEAD_DIM]], axis=0)
                v2 = jnp.concatenate([prev_ref[:, SWA_KV + hk * HEAD_DIM:SWA_KV + (hk + 1) * HEAD_DIM],
                                      cur_ref[0:WINDOW, vc + hk * HEAD_DIM:vc + (hk + 1) * HEAD_DIM]], axis=0)
                mask = band & ((kj >= WINDOW) | (n > 0))
            else:
                k2 = cur_ref[r0 - WINDOW:r0 + WINDOW, kc + hk * HEAD_DIM:kc + (hk + 1) * HEAD_DIM]
                v2 = cur_ref[r0 - WINDOW:r0 + WINDOW, vc + hk * HEAD_DIM:vc + (hk + 1) * HEAD_DIM]
                mask = band
            heads = [hk * SWA_GROUP + g for g in range(SWA_GROUP)]
            q4 = jnp.concatenate(
                [cur_ref[r0:r0 + WINDOW, hd * HEAD_DIM:(hd + 1) * HEAD_DIM] for hd in heads], axis=0)
            s = _dot_nt(q4, k2) * HEAD_DIM ** -0.5
            s = s + bias_ref[hk * SWA_GROUP:(hk + 1) * SWA_GROUP].reshape(rows, 2 * WINDOW)
            s = jnp.where(mask, s, NEG_INF)
            sink = jnp.concatenate([jnp.full((WINDOW, 1), sinks_ref[hd], F32) for hd in heads], axis=0)
            m = jnp.maximum(jnp.max(s, axis=-1, keepdims=True), sink)
            p = jnp.exp(s - m)
            denom = jnp.sum(p, axis=-1, keepdims=True) + jnp.exp(sink - m)
            o = _dot(p.astype(BF16), v2) / denom
            for g, hd in enumerate(heads):
                o_ref[r0:r0 + WINDOW, hd * HEAD_DIM:(hd + 1) * HEAD_DIM] = o[g * WINDOW:(g + 1) * WINDOW].astype(BF16)


def _swa(swa3, sinks, bias, tq=512):
    b, t, wd = swa3.shape
    nblk = tq // WINDOW
    return pl.pallas_call(
        _swa_kernel,
        grid=(b, t // tq),
        in_specs=[pl.BlockSpec(memory_space=pltpu.SMEM),
                  pl.BlockSpec((None, tq, wd), lambda bi, n: (bi, n, 0)),
                  pl.BlockSpec((None, WINDOW, 2 * SWA_KV), lambda bi, n: (bi, jnp.maximum(n * nblk - 1, 0), SWA_Q // (2 * SWA_KV))),
                  _const_spec((SWA_Q_HEADS, WINDOW, 2 * WINDOW))],
        out_specs=pl.BlockSpec((None, tq, SWA_Q), lambda bi, n: (bi, n, 0)),
        out_shape=jax.ShapeDtypeStruct((b, t, SWA_Q), BF16),
        compiler_params=pltpu.CompilerParams(dimension_semantics=("arbitrary", "arbitrary")),
        name="swa_attention",
    )(sinks.astype(F32), swa3, swa3, bias)


def _split3(x):
    hi = x.astype(BF16)
    r = x - hi.astype(F32)
    mid = r.astype(BF16)
    lo = (r - mid.astype(F32)).astype(BF16)
    return hi, mid, lo


def _log_sigmoid(x):
    return jnp.minimum(x, 0.0) - jnp.log1p(jnp.exp(-jnp.abs(x)))


def _mlstm_kernel(qk_ref, v_ref, so_ref, mif_ref, ng_ref, o_ref, c_ref, n_ref, m_ref, mift_ref):
    L = MLSTM_CHUNK
    H, Dh = MLSTM_HEADS, MLSTM_HEAD_DIM
    ts = qk_ref.shape[0]

    @pl.when(pl.program_id(1) == 0)
    def _():
        c_ref[...] = jnp.zeros(c_ref.shape, F32)
        n_ref[...] = jnp.zeros(n_ref.shape, F32)
        m_ref[...] = jnp.zeros(m_ref.shape, F32)

    mift_ref[...] = mif_ref[...].T

    ti = lax.broadcasted_iota(jnp.int32, (L, L), 0)
    si = lax.broadcasted_iota(jnp.int32, (L, L), 1)
    causal = si <= ti
    tri_l = causal.astype(BF16)
    tri_u = (ti <= si).astype(BF16)

    def chunk(c, carry):
        r0 = pl.multiple_of(c * L, L)
        rows = pl.ds(r0, L)
        mif_c = mif_ref[rows, :]
        ig_col = mif_c[:, 0:H]
        lf_col = _log_sigmoid(mif_c[:, H:2 * H])
        gates_t = mift_ref[0:2 * H, rows]
        ig_row = gates_t[0:H]
        lf_row = _log_sigmoid(gates_t[H:2 * H])
        a_col = sum(_dot(tri_l, part) for part in _split3(lf_col))
        a_row = sum(_dot(part, tri_u) for part in _split3(lf_row))
        u_col = ig_col - a_col
        u_row = ig_row - a_row
        umax = jnp.max(u_row, axis=1, keepdims=True)
        g_all = a_row[:, L - 1:L]
        m_all = m_ref[:, 0:1]
        m_loc = g_all + umax
        m_new = jnp.maximum(g_all + m_all, m_loc)
        decay = jnp.exp(g_all + m_all - m_new)
        inject = jnp.exp(m_loc - m_new)
        m_ref[:, 0:1] = m_new

        for h in range(H):
            hc = slice(h * Dh, (h + 1) * Dh)
            q = qk_ref[rows, hc]
            k = qk_ref[rows, MLSTM_W + h * Dh:MLSTM_W + (h + 1) * Dh]
            v = v_ref[rows, hc]
            m_in = m_all[h:h + 1, :]
            u_r = u_row[h:h + 1, :]
            c_in = c_ref[h]
            n_in = n_ref[h:h + 1, :]
            big_m = jnp.maximum(jnp.max(jnp.where(causal, u_r, NEG_INF), axis=1, keepdims=True), m_in)
            dmat = jnp.exp(jnp.where(causal, u_r - big_m, NEG_INF))
            s = _dot_nt(q, k) * dmat
            inter_w = jnp.exp(m_in - big_m)
            num = _dot(s.astype(BF16), v) + inter_w * _dot(q, c_in.astype(BF16))
            den = (jnp.sum(s, axis=1, keepdims=True)
                   + inter_w * jnp.sum(q.astype(F32) * n_in, axis=1, keepdims=True))
            floor = jnp.exp(-(a_col[:, h:h + 1] + big_m))
            hh = num / jnp.maximum(jnp.abs(den), floor)
            mu = jnp.mean(hh, axis=-1, keepdims=True)
            ctr = hh - mu
            var = jnp.mean(ctr * ctr, axis=-1, keepdims=True)
            y = so_ref[rows, hc].astype(F32) * (ctr * lax.rsqrt(var + EPS) * ng_ref[:, hc])
            o_ref[rows, hc] = y.astype(BF16)
            wk = jnp.exp(u_col[:, h:h + 1] - umax[h:h + 1, :]) * k.astype(F32)
            d_c = pl.dot(wk.astype(BF16), v, trans_a=True)
            d_n = jnp.sum(wk, axis=0, keepdims=True)
            dec = decay[h:h + 1, :]
            inj = inject[h:h + 1, :]
            c_ref[h] = dec * c_in + inj * d_c
            n_ref[h:h + 1, :] = dec * n_in + inj * d_n
        return carry

    lax.fori_loop(0, ts // L, chunk, 0)


def _mlstm(qk3, v3, so3, mif3, norm_g, ts=512):
    b, t, _ = qk3.shape
    H, Dh = MLSTM_HEADS, MLSTM_HEAD_DIM
    blk = lambda wd: pl.BlockSpec((None, ts, wd), lambda bi, j: (bi, j, 0))
    return pl.pallas_call(
        _mlstm_kernel,
        grid=(b, t // ts),
        in_specs=[blk(2 * MLSTM_W), blk(MLSTM_W), blk(MLSTM_W), blk(IF_PAD), _const_spec((1, MLSTM_W))],
        out_specs=blk(MLSTM_W),
        out_shape=jax.ShapeDtypeStruct((b, t, MLSTM_W), BF16),
        scratch_shapes=[pltpu.VMEM((H, Dh, Dh), F32), pltpu.VMEM((H, Dh), F32),
                        pltpu.VMEM((H, 128), F32), pltpu.VMEM((IF_PAD, ts), F32)],
        compiler_params=pltpu.CompilerParams(dimension_semantics=("arbitrary", "arbitrary")),
        name="mlstm",
    )(qk3, v3, so3, mif3, norm_g)


def _norm_matmul_kernel(x_ref, g_ref, w_ref, o_ref):
    o_ref[...] = _dot(_rms(x_ref[...], g_ref[...]).astype(BF16), w_ref[...]).astype(o_ref.dtype)


def _norm_matmul(x2, g, w, tm=512):
    n, d = x2.shape
    nout = w.shape[1]
    return pl.pallas_call(
        _norm_matmul_kernel,
        grid=(n // tm,),
        in_specs=[pl.BlockSpec((tm, d), lambda i: (i, 0)), _const_spec((1, d)), _const_spec((d, nout))],
        out_specs=pl.BlockSpec((tm, nout), lambda i: (i, 0)),
        out_shape=jax.ShapeDtypeStruct((n, nout), BF16),
        compiler_params=pltpu.CompilerParams(dimension_semantics=("arbitrary",)),
        name="mem_kv_proj",
    )(x2, g, w)


def _xattn_kernel(q_ref, kv_ref, o_ref):
    for h in range(X_HEADS):
        hc = slice(h * X_HEAD_DIM, (h + 1) * X_HEAD_DIM)
        s = _dot_nt(q_ref[:, hc], kv_ref[:, hc]) * X_HEAD_DIM ** -0.5
        p = jnp.exp(s - jnp.max(s, axis=-1, keepdims=True))
        denom = jnp.sum(p, axis=-1, keepdims=True)
        o = _dot(p.astype(BF16), kv_ref[:, X_W + h * X_HEAD_DIM:X_W + (h + 1) * X_HEAD_DIM]) / denom
        o_ref[:, hc] = o.astype(BF16)


def _xattn(xq3, kv3, tq=512):
    b, t, _ = xq3.shape
    m = kv3.shape[1]
    return pl.pallas_call(
        _xattn_kernel,
        grid=(b, t // tq),
        in_specs=[pl.BlockSpec((None, tq, X_W), lambda bi, j: (bi, j, 0)),
                  pl.BlockSpec((None, m, 2 * X_W), lambda bi, j: (bi, 0, 0))],
        out_specs=pl.BlockSpec((None, tq, X_W), lambda bi, j: (bi, j, 0)),
        out_shape=jax.ShapeDtypeStruct((b, t, X_W), BF16),
        compiler_params=pltpu.CompilerParams(dimension_semantics=("arbitrary", "arbitrary")),
        name="cross_attention",
    )(xq3, kv3)


def _post_kernel(x_ref, ys_ref, ym_ref, yx_ref, gate_ref, wbs_ref, wbm_ref, wbx_ref, wout_ref,
                 gffn_ref, w1_ref, w2_ref, gfin_ref, o_ref, hid_ref, *, final_norm):
    def branch(y_ref, w_ref, k):
        return gate_ref[:, k * D_MODEL:(k + 1) * D_MODEL].astype(F32) * _dot(y_ref[...], w_ref[...])

    merged = branch(ys_ref, wbs_ref, 0) + branch(ym_ref, wbm_ref, 1) + branch(yx_ref, wbx_ref, 2)
    x1 = x_ref[...] + _dot(merged.astype(BF16), wout_ref[...])
    h = _rms(x1, gffn_ref[...]).astype(BF16)
    step = D_MODEL
    for c in range(D_FF // step):
        a = jnp.maximum(_dot(h, w1_ref[:, c * step:(c + 1) * step]), 0.0)
        hid_ref[:, c * step:(c + 1) * step] = (a * a).astype(BF16)
    x2 = x1 + _dot(hid_ref[...], w2_ref[...])
    if final_norm:
        x2 = _rms(x2, gfin_ref[...])
    o_ref[...] = x2


def _post(x2, ys, ym, yx, gates, wbs, wbm, wbx, wout, gffn, w1, w2, gfin, final_norm, tm=256):
    n = x2.shape[0]
    row = lambda wd: pl.BlockSpec((tm, wd), lambda i: (i, 0))
    return pl.pallas_call(
        functools.partial(_post_kernel, final_norm=final_norm),
        grid=(n // tm,),
        in_specs=[row(D_MODEL), row(SWA_Q), row(MLSTM_W), row(X_W), row(N_BRANCH * D_MODEL),
                  _const_spec((SWA_Q, D_MODEL)), _const_spec((MLSTM_W, D_MODEL)), _const_spec((X_W, D_MODEL)),
                  _const_spec((D_MODEL, D_MODEL)), _const_spec((1, D_MODEL)),
                  _const_spec((D_MODEL, D_FF)), _const_spec((D_FF, D_MODEL)), _const_spec((1, D_MODEL))],
        out_specs=row(D_MODEL),
        out_shape=jax.ShapeDtypeStruct((n, D_MODEL), F32),
        scratch_shapes=[pltpu.VMEM((tm, D_FF), BF16)],
        compiler_params=pltpu.CompilerParams(
            dimension_semantics=("arbitrary",), vmem_limit_bytes=VMEM_LIMIT),
        name="merge_out_mlp",
    )(x2, ys, ym, yx, gates, wbs, wbm, wbx, wout, gffn, w1, w2, gfin)


def _prep_in_weights(w_in_l, b_i_l, b_f_l):
    o_sq = 0
    o_mq = o_sq + SWA_Q + 2 * SWA_KV
    o_mv = o_mq + 2 * MLSTM_W
    o_mi = o_mv + MLSTM_W
    o_mo = o_mi + 2 * MLSTM_HEADS
    o_xq = o_mo + MLSTM_W
    w_main = jnp.concatenate([w_in_l[:, :o_mi], w_in_l[:, o_mo:]], axis=1).astype(BF16)
    pad = IF_PAD - 2 * MLSTM_HEADS
    w_if = jnp.pad(w_in_l[:, o_mi:o_mo], ((0, 0), (0, pad))).astype(BF16)
    b_if = jnp.pad(jnp.concatenate([b_i_l, b_f_l]), (0, pad)).astype(F32)[None, :]
    del o_xq
    return w_main, w_if, b_if


def kernel(x, mem, rel_bias, g_mix, w_in, conv_w, b_i, b_f, mlstm_norm_g, sinks, g_mem, w_mem_kv,
           w_br_swa, w_br_mlstm, w_br_x, w_out, g_ffn, w_ff1, w_ff2, g_final):
    bsz, seq, d = x.shape
    mlen = mem.shape[1]
    depth = w_in.shape[0]
    n = bsz * seq
    bias = _bias_table(rel_bias)
    x2 = x.reshape(n, d).astype(F32)
    mem2 = mem.reshape(bsz * mlen, d).astype(F32)
    row = lambda v: v.astype(F32)[None, :]
    for l in range(depth):
        w_main, w_if, b_if = _prep_in_weights(w_in[l], b_i[l], b_f[l])
        swa, mqk, mv, mo, xq, gates, mif = _in_proj(
            x2, row(g_mix[l]), w_main, w_if, b_if, conv_w[l].astype(F32), seq)
        y_swa = _swa(swa.reshape(bsz, seq, -1), sinks[l], bias)
        y_ml = _mlstm(mqk.reshape(bsz, seq, -1), mv.reshape(bsz, seq, -1), mo.reshape(bsz, seq, -1),
                      mif.reshape(bsz, seq, -1), row(mlstm_norm_g[l]))
        mem_kv = _norm_matmul(mem2, row(g_mem[l]), w_mem_kv[l].astype(BF16))
        y_x = _xattn(xq.reshape(bsz, seq, -1), mem_kv.reshape(bsz, mlen, -1))
        x2 = _post(x2, y_swa.reshape(n, -1), y_ml.reshape(n, -1), y_x.reshape(n, -1), gates,
                   w_br_swa[l].astype(BF16), w_br_mlstm[l].astype(BF16), w_br_x[l].astype(BF16),
                   w_out[l].astype(BF16), row(g_ffn[l]), w_ff1[l].astype(BF16), w_ff2[l].astype(BF16),
                   row(g_final), final_norm=(l == depth - 1))
    return x2.reshape(bsz, seq, d).astype(x.dtype)
```

```python
import functools
import math

import numpy as np
import jax
import jax.numpy as jnp
from jax import lax
from jax.experimental import pallas as pl
from jax.experimental.pallas import tpu as pltpu

D_MODEL = 1024
HEAD_DIM = 64
SWA_Q_HEADS = 8
SWA_KV_HEADS = 2
SWA_GROUP = SWA_Q_HEADS // SWA_KV_HEADS
WINDOW = 128
MLSTM_HEADS = 8
MLSTM_HEAD_DIM = 64
CONV_WIDTH = 4
X_HEADS = 4
X_HEAD_DIM = 128
N_BRANCH = 3
D_FF = 4 * D_MODEL
REL_BUCKETS = 32
REL_MAX_EXACT = 16
REL_MAX_DIST = 128
EPS = 1e-6
NEG_INF = -1e30

SWA_Q = SWA_Q_HEADS * HEAD_DIM
SWA_KV = SWA_KV_HEADS * HEAD_DIM
MLSTM_W = MLSTM_HEADS * MLSTM_HEAD_DIM
X_W = X_HEADS * X_HEAD_DIM
LANES = 128

SWA_QK_W = SWA_Q + SWA_KV_HEADS * LANES
C_SWA = 0
C_MQK = C_SWA + SWA_QK_W
C_XQ = C_MQK + 2 * MLSTM_W
C_GATE = C_XQ + X_W
C_END = C_GATE + N_BRANCH * D_MODEL
R_SV = 0
R_MV = R_SV + SWA_KV
R_MO = R_MV + MLSTM_W
R_END = R_MO + MLSTM_W
N_IF = 2 * MLSTM_HEADS

MLSTM_CHUNK = 128
STATE_ROWS = MLSTM_HEAD_DIM + 16
CONV_HALO = 8

VMEM_LIMIT = 56 * 1024 * 1024

BF16 = jnp.bfloat16
F32 = jnp.float32


def _rms(x, g):
    return x * lax.rsqrt(jnp.mean(x * x, axis=-1, keepdims=True) + EPS) * g


def _const_spec(shape):
    nd = len(shape)
    return pl.BlockSpec(shape, lambda *_: (0,) * nd, pipeline_mode=pl.Buffered(1))


def _dot(a, b):
    return jnp.dot(a, b, preferred_element_type=F32)


def _dot_nt(a, b):
    return lax.dot_general(a, b, (((1,), (1,)), ((), ())), preferred_element_type=F32)


def _dot_tn(a, b):
    return lax.dot_general(a, b, (((0,), (0,)), ((), ())), preferred_element_type=F32)


def _in_proj_kernel(x_ref, g_ref, w_ref, wt_ref, wift_ref, bift_ref, convw_ref,
                    swa_ref, mqk_ref, xq_ref, gate_ref, svt_ref, mvt_ref, sot_ref, gt_ref,
                    conv_buf, *, blocks_per_seq):
    i = pl.program_id(0)
    tm = x_ref.shape[0]
    h = _rms(x_ref[...], g_ref[...]).astype(BF16)

    def proj(lo, width):
        return _dot(h, w_ref[:, lo:lo + width])

    def proj_t(lo, rows):
        return _dot_nt(wt_ref[lo:lo + rows, :], h)

    swa_ref[...] = proj(C_SWA, SWA_QK_W).astype(BF16)
    xq_ref[...] = proj(C_XQ, X_W).astype(BF16)
    for k in range(N_BRANCH):
        gate_ref[:, k * D_MODEL:(k + 1) * D_MODEL] = jax.nn.sigmoid(
            proj(C_GATE + k * D_MODEL, D_MODEL)).astype(BF16)
    svt_ref[...] = proj_t(R_SV, SWA_KV).astype(BF16)
    mvt_ref[...] = proj_t(R_MV, MLSTM_W).astype(BF16)
    sot_ref[...] = jax.nn.sigmoid(proj_t(R_MO, MLSTM_W)).astype(BF16)
    gt_ref[...] = _dot_nt(wift_ref[...], h) + bift_ref[...]

    @pl.when(i % blocks_per_seq == 0)
    def _():
        conv_buf[0:CONV_HALO, :] = jnp.zeros((CONV_HALO, 2 * MLSTM_W), F32)

    @pl.when(i % blocks_per_seq != 0)
    def _():
        conv_buf[0:CONV_HALO, :] = conv_buf[tm:tm + CONV_HALO, :]

    conv_buf[CONV_HALO:CONV_HALO + tm, :] = proj(C_MQK, 2 * MLSTM_W)
    for part, scale in ((0, 1.0), (1, MLSTM_HEAD_DIM ** -0.5)):
        cols = slice(part * MLSTM_W, (part + 1) * MLSTM_W)
        acc = None
        for j in range(CONV_WIDTH):
            r0 = CONV_HALO - (CONV_WIDTH - 1) + j
            term = convw_ref[j:j + 1, cols] * conv_buf[r0:r0 + tm, cols]
            acc = term if acc is None else acc + term
        act = acc * jax.nn.sigmoid(acc)
        mqk_ref[:, cols] = (act * scale).astype(BF16)


def _in_proj(x2, g, w, wt, wift, bift, convw, seq_len, tm=512):
    n = x2.shape[0]
    assert n % tm == 0 and seq_len % tm == 0
    row = lambda width: pl.BlockSpec((tm, width), lambda i: (i, 0))
    col = lambda rows: pl.BlockSpec((rows, tm), lambda i: (0, i))
    tok_widths = (SWA_QK_W, 2 * MLSTM_W, X_W, N_BRANCH * D_MODEL)
    feat_rows = (SWA_KV, MLSTM_W, MLSTM_W)
    out_shape = ([jax.ShapeDtypeStruct((n, wd), BF16) for wd in tok_widths]
                 + [jax.ShapeDtypeStruct((r, n), BF16) for r in feat_rows]
                 + [jax.ShapeDtypeStruct((N_IF, n), F32)])
    return pl.pallas_call(
        functools.partial(_in_proj_kernel, blocks_per_seq=seq_len // tm),
        grid=(n // tm,),
        in_specs=[row(D_MODEL), _const_spec((1, D_MODEL)), _const_spec((D_MODEL, C_END)),
                  _const_spec((R_END, D_MODEL)), _const_spec((N_IF, D_MODEL)), _const_spec((N_IF, 1)),
                  _const_spec((CONV_WIDTH, 2 * MLSTM_W))],
        out_specs=[row(wd) for wd in tok_widths] + [col(r) for r in feat_rows] + [col(N_IF)],
        out_shape=out_shape,
        scratch_shapes=[pltpu.VMEM((tm + CONV_HALO, 2 * MLSTM_W), F32)],
        compiler_params=pltpu.CompilerParams(
            dimension_semantics=("arbitrary",), vmem_limit_bytes=VMEM_LIMIT),
        name="in_proj",
    )(x2, g, w, wt, wift, bift, convw)


def _t5_bucket_table_t():
    qi = np.arange(WINDOW)[:, None]
    kj = np.arange(2 * WINDOW)[None, :]
    dist = qi + WINDOW - kj
    n = np.maximum(dist, 0)
    nf = np.maximum(n, 1).astype(np.float32)
    scale = np.float32((REL_BUCKETS - REL_MAX_EXACT) / math.log(REL_MAX_DIST / REL_MAX_EXACT))
    large = REL_MAX_EXACT + (np.log(nf / np.float32(REL_MAX_EXACT)) * scale).astype(np.int32)
    large = np.minimum(large, REL_BUCKETS - 1)
    bucket = np.where(n < REL_MAX_EXACT, n, large)
    band = (dist >= 0) & (dist < WINDOW)
    return np.where(band, bucket, -1).astype(np.int32).T.copy()


def _bias_kernel(rel_ref, bucket_ref, o_ref):
    bucket = bucket_ref[...]
    for h in range(SWA_Q_HEADS):
        acc = jnp.full(bucket.shape, NEG_INF, F32)
        for b in range(REL_BUCKETS):
            acc = jnp.where(bucket == b, rel_ref[b, h], acc)
        hk, g = divmod(h, SWA_GROUP)
        o_ref[hk, :, g * WINDOW:(g + 1) * WINDOW] = acc


def _bias_table(rel_bias):
    bucket = jnp.asarray(_t5_bucket_table_t())
    shape = (SWA_KV_HEADS, 2 * WINDOW, SWA_GROUP * WINDOW)
    return pl.pallas_call(
        _bias_kernel,
        in_specs=[pl.BlockSpec(memory_space=pltpu.SMEM),
                  pl.BlockSpec((2 * WINDOW, WINDOW), lambda: (0, 0))],
        out_specs=pl.BlockSpec(shape, lambda: (0, 0, 0)),
        out_shape=jax.ShapeDtypeStruct(shape, F32),
        name="t5_bias_table",
    )(rel_bias.astype(F32), bucket)


def _swa_kernel(sinks_ref, qk_ref, kprev_ref, vt_ref, vtprev_ref, bias_ref, o_ref):
    n = pl.program_id(1)
    tq = qk_ref.shape[0]
    cols = SWA_GROUP * WINDOW
    kj = lax.broadcasted_iota(jnp.int32, (2 * WINDOW, cols), 0)
    lo_half = lax.broadcasted_iota(jnp.int32, (WINDOW, LANES), 1) < HEAD_DIM
    for blk in range(tq // WINDOW):
        r0 = blk * WINDOW
        for hk in range(SWA_KV_HEADS):
            kc = slice(SWA_Q + hk * LANES, SWA_Q + (hk + 1) * LANES)
            vr = slice(hk * HEAD_DIM, (hk + 1) * HEAD_DIM)
            if blk == 0:
                k2 = jnp.concatenate([kprev_ref[:, hk * LANES:(hk + 1) * LANES], qk_ref[0:WINDOW, kc]], axis=0)
                v2t = jnp.concatenate([vtprev_ref[vr, :], vt_ref[vr, 0:WINDOW]], axis=1)
            else:
                k2 = qk_ref[r0 - WINDOW:r0 + WINDOW, kc]
                v2t = vt_ref[vr, r0 - WINDOW:r0 + WINDOW]
            tiles = []
            for pair in (2 * hk, 2 * hk + 1):
                t = qk_ref[r0:r0 + WINDOW, pair * LANES:(pair + 1) * LANES]
                tiles += [jnp.where(lo_half, t, 0), jnp.where(lo_half, 0, t)]
            q4 = jnp.concatenate(tiles, axis=0)
            s = _dot_nt(k2, q4) * HEAD_DIM ** -0.5 + bias_ref[hk]
            if blk == 0:
                s = jnp.where((kj >= WINDOW) | (n > 0), s, NEG_INF)
            heads = [hk * SWA_GROUP + g for g in range(SWA_GROUP)]
            sink = jnp.concatenate([jnp.full((1, WINDOW), sinks_ref[hd], F32) for hd in heads], axis=1)
            m = jnp.maximum(jnp.max(s, axis=0, keepdims=True), sink)
            p = jnp.exp(s - m)
            denom = jnp.sum(p, axis=0, keepdims=True) + jnp.exp(sink - m)
            o = _dot(v2t, p.astype(BF16)) / denom
            for g, hd in enumerate(heads):
                o_ref[hd * HEAD_DIM:(hd + 1) * HEAD_DIM, r0:r0 + WINDOW] = (
                    o[:, g * WINDOW:(g + 1) * WINDOW].astype(BF16))


def _swa(swa_qk3, svt, sinks, bias, tq=512):
    b, t, wd = swa_qk3.shape
    nblk = tq // WINDOW
    steps = t // tq
    return pl.pallas_call(
        _swa_kernel,
        grid=(b, steps),
        in_specs=[pl.BlockSpec(memory_space=pltpu.SMEM),
                  pl.BlockSpec((None, tq, wd), lambda bi, n: (bi, n, 0)),
                  pl.BlockSpec((None, WINDOW, SWA_KV_HEADS * LANES),
                               lambda bi, n: (bi, jnp.maximum(n * nblk - 1, 0), SWA_Q // (SWA_KV_HEADS * LANES))),
                  pl.BlockSpec((SWA_KV, tq), lambda bi, n: (0, bi * steps + n)),
                  pl.BlockSpec((SWA_KV, WINDOW), lambda bi, n: (0, jnp.maximum((bi * steps + n) * nblk - 1, 0))),
                  _const_spec((SWA_KV_HEADS, 2 * WINDOW, SWA_GROUP * WINDOW))],
        out_specs=pl.BlockSpec((SWA_Q, tq), lambda bi, n: (0, bi * steps + n)),
        out_shape=jax.ShapeDtypeStruct((SWA_Q, b * t), BF16),
        compiler_params=pltpu.CompilerParams(dimension_semantics=("arbitrary", "arbitrary")),
        name="swa_attention",
    )(sinks.astype(F32), swa_qk3, swa_qk3, svt, svt, bias)


def _split3(x):
    hi = x.astype(BF16).astype(F32)
    r = x - hi
    mid = r.astype(BF16).astype(F32)
    lo = (r - mid).astype(BF16).astype(F32)
    return [hi, mid, lo]


def _log_sigmoid(x):
    return jnp.minimum(x, 0.0) - jnp.log1p(jnp.exp(-jnp.abs(x)))


def _mlstm_kernel(qk_ref, vt_ref, sot_ref, gt_ref, ngb_ref, o_ref, state_ref, m_ref):
    L = MLSTM_CHUNK
    H, Dh = MLSTM_HEADS, MLSTM_HEAD_DIM
    ts = qk_ref.shape[0]

    @pl.when(pl.program_id(1) == 0)
    def _():
        state_ref[...] = jnp.zeros(state_ref.shape, F32)
        m_ref[...] = jnp.zeros(m_ref.shape, F32)

    lane_h = lax.broadcasted_iota(jnp.int32, (H, L), 1)
    s_i = lax.broadcasted_iota(jnp.int32, (L, L), 0)
    t_i = lax.broadcasted_iota(jnp.int32, (L, L), 1)
    causal_t = s_i <= t_i
    tri_u = causal_t.astype(BF16)
    lo_half = lax.broadcasted_iota(jnp.int32, (L, LANES), 1) < Dh
    lo_half_state = lax.broadcasted_iota(jnp.int32, (STATE_ROWS, LANES), 1) < Dh
    n_row = lax.broadcasted_iota(jnp.int32, (STATE_ROWS - Dh, L), 0) == 0
    e_rows = 6 * H
    block_sel = (lax.broadcasted_iota(jnp.int32, (e_rows, H * L), 0) % H
                 == lax.broadcasted_iota(jnp.int32, (e_rows, H * L), 1) // L)

    def chunk(c, carry):
        r0 = pl.multiple_of(c * L, L)
        rows = pl.ds(r0, L)
        gates = gt_ref[:, rows]
        ig = gates[0:H]
        lf = _log_sigmoid(gates[H:2 * H])
        a = sum(_dot(part.astype(BF16), tri_u) for part in _split3(lf))
        u = ig - a
        cm = u
        sh = 1
        while sh < L:
            cm = jnp.maximum(cm, jnp.where(lane_h >= sh, pltpu.roll(cm, sh, 1), NEG_INF))
            sh *= 2
        m_in = m_ref[:, 0:1]
        big_m = jnp.maximum(cm, m_in)
        umax = cm[:, L - 1:L]
        g_all = a[:, L - 1:L]
        w = jnp.exp(u - umax)
        inter_w = jnp.exp(m_in - big_m)
        floor = jnp.exp(-(a + big_m))
        m_loc = g_all + umax
        m_new = jnp.maximum(g_all + m_in, m_loc)
        decay = jnp.exp(g_all + m_in - m_new)
        inject = jnp.exp(m_loc - m_new)
        m_ref[...] = jnp.broadcast_to(m_new, m_ref.shape)

        at = jnp.concatenate(_split3(u) + [jnp.ones((3 * H, L), F32)], axis=0)
        nm = jnp.concatenate(_split3(-big_m), axis=0)
        bb = jnp.concatenate([jnp.ones((3 * H, H * L), F32), jnp.concatenate([nm] * H, axis=1)], axis=0)
        e = _dot_tn(at.astype(BF16), jnp.where(block_sel, bb, 0.0).astype(BF16))

        for h in range(H):
            pair = h // 2
            hr = slice(h * Dh, (h + 1) * Dh)
            own = lo_half if h % 2 == 0 else ~lo_half
            own_state = lo_half_state if h % 2 == 0 else ~lo_half_state
            qz = jnp.where(own, qk_ref[rows, pair * LANES:(pair + 1) * LANES], 0)
            kp = qk_ref[rows, MLSTM_W + pair * LANES:MLSTM_W + (pair + 1) * LANES]
            vt = vt_ref[hr, rows]
            state = state_ref[h]
            d_t = jnp.exp(jnp.where(causal_t, e[:, h * L:(h + 1) * L], NEG_INF))
            p = _dot_nt(kp, qz) * d_t
            inter = _dot_nt(state.astype(BF16), qz)
            iw = inter_w[h:h + 1]
            num = _dot(vt, p.astype(BF16)) + iw * inter[0:Dh]
            den = jnp.sum(p, axis=0, keepdims=True) + iw * inter[Dh:Dh + 1]
            hh = num / jnp.maximum(jnp.abs(den), floor[h:h + 1])
            mu = jnp.mean(hh, axis=0, keepdims=True)
            ctr = hh - mu
            var = jnp.mean(ctr * ctr, axis=0, keepdims=True)
            y = sot_ref[hr, rows].astype(F32) * (ctr * lax.rsqrt(var + EPS) * ngb_ref[hr, :])
            o_ref[hr, rows] = y.astype(BF16)
            w_r = w[h:h + 1]
            vaug = jnp.concatenate([vt.astype(F32) * w_r, jnp.where(n_row, w_r, 0.0)], axis=0)
            d_state = jnp.where(own_state, _dot(vaug.astype(BF16), kp), 0.0)
            state_ref[h] = decay[h:h + 1] * state + inject[h:h + 1] * d_state
        return carry

    lax.fori_loop(0, ts // L, chunk, 0)


def _mlstm(qk3, mvt, sot, gt, norm_g_b, ts=512):
    b, t, _ = qk3.shape
    steps = t // ts
    col = lambda rows: pl.BlockSpec((rows, ts), lambda bi, j: (0, bi * steps + j))
    return pl.pallas_call(
        _mlstm_kernel,
        grid=(b, steps),
        in_specs=[pl.BlockSpec((None, ts, 2 * MLSTM_W), lambda bi, j: (bi, j, 0)),
                  col(MLSTM_W), col(MLSTM_W), col(N_IF), _const_spec((MLSTM_W, LANES))],
        out_specs=col(MLSTM_W),
        out_shape=jax.ShapeDtypeStruct((MLSTM_W, b * t), BF16),
        scratch_shapes=[pltpu.VMEM((MLSTM_HEADS, STATE_ROWS, LANES), F32),
                        pltpu.VMEM((MLSTM_HEADS, LANES), F32)],
        compiler_params=pltpu.CompilerParams(dimension_semantics=("arbitrary", "arbitrary")),
        name="mlstm",
    )(qk3, mvt, sot, gt, norm_g_b)


def _norm_matmul_kernel(x_ref, g_ref, w_ref, o_ref):
    o_ref[...] = _dot(_rms(x_ref[...], g_ref[...]).astype(BF16), w_ref[...]).astype(o_ref.dtype)


def _norm_matmul(x2, g, w, tm=512):
    n, d = x2.shape
    nout = w.shape[1]
    return pl.pallas_call(
        _norm_matmul_kernel,
        grid=(n // tm,),
        in_specs=[pl.BlockSpec((tm, d), lambda i: (i, 0)), _const_spec((1, d)), _const_spec((d, nout))],
        out_specs=pl.BlockSpec((tm, nout), lambda i: (i, 0)),
        out_shape=jax.ShapeDtypeStruct((n, nout), BF16),
        compiler_params=pltpu.CompilerParams(dimension_semantics=("arbitrary",)),
        name="mem_kv_proj",
    )(x2, g, w)


def _xattn_kernel(q_ref, kv_ref, o_ref):
    for h in range(X_HEADS):
        hc = slice(h * X_HEAD_DIM, (h + 1) * X_HEAD_DIM)
        s = _dot_nt(q_ref[:, hc], kv_ref[:, hc]) * X_HEAD_DIM ** -0.5
        p = jnp.exp(s - jnp.max(s, axis=-1, keepdims=True))
        denom = jnp.sum(p, axis=-1, keepdims=True)
        o = _dot(p.astype(BF16), kv_ref[:, X_W + h * X_HEAD_DIM:X_W + (h + 1) * X_HEAD_DIM]) / denom
        o_ref[:, hc] = o.astype(BF16)


def _xattn(xq3, kv3, tq=512):
    b, t, _ = xq3.shape
    m = kv3.shape[1]
    return pl.pallas_call(
        _xattn_kernel,
        grid=(b, t // tq),
        in_specs=[pl.BlockSpec((None, tq, X_W), lambda bi, j: (bi, j, 0)),
                  pl.BlockSpec((None, m, 2 * X_W), lambda bi, j: (bi, 0, 0))],
        out_specs=pl.BlockSpec((None, tq, X_W), lambda bi, j: (bi, j, 0)),
        out_shape=jax.ShapeDtypeStruct((b, t, X_W), BF16),
        compiler_params=pltpu.CompilerParams(dimension_semantics=("arbitrary", "arbitrary")),
        name="cross_attention",
    )(xq3, kv3)


def _post_kernel(x_ref, yst_ref, ymt_ref, yx_ref, gate_ref, wbs_ref, wbm_ref, wbx_ref, wout_ref,
                 gffn_ref, w1_ref, w2_ref, gfin_ref, o_ref, hid_ref, *, final_norm):
    def gate(k):
        return gate_ref[:, k * D_MODEL:(k + 1) * D_MODEL].astype(F32)

    merged = (gate(0) * _dot_tn(yst_ref[...], wbs_ref[...])
              + gate(1) * _dot_tn(ymt_ref[...], wbm_ref[...])
              + gate(2) * _dot(yx_ref[...], wbx_ref[...]))
    x1 = x_ref[...] + _dot(merged.astype(BF16), wout_ref[...])
    h = _rms(x1, gffn_ref[...]).astype(BF16)
    step = D_MODEL
    for c in range(D_FF // step):
        a = jnp.maximum(_dot(h, w1_ref[:, c * step:(c + 1) * step]), 0.0)
        hid_ref[:, c * step:(c + 1) * step] = (a * a).astype(BF16)
    x2 = x1 + _dot(hid_ref[...], w2_ref[...])
    if final_norm:
        x2 = _rms(x2, gfin_ref[...])
    o_ref[...] = x2


def _post(x2, yst, ymt, yx, gates, wbs, wbm, wbx, wout, gffn, w1, w2, gfin, final_norm, tm=256):
    n = x2.shape[0]
    row = lambda wd: pl.BlockSpec((tm, wd), lambda i: (i, 0))
    col = lambda rows: pl.BlockSpec((rows, tm), lambda i: (0, i))
    return pl.pallas_call(
        functools.partial(_post_kernel, final_norm=final_norm),
        grid=(n // tm,),
        in_specs=[row(D_MODEL), col(SWA_Q), col(MLSTM_W), row(X_W), row(N_BRANCH * D_MODEL),
                  _const_spec((SWA_Q, D_MODEL)), _const_spec((MLSTM_W, D_MODEL)), _const_spec((X_W, D_MODEL)),
                  _const_spec((D_MODEL, D_MODEL)), _const_spec((1, D_MODEL)),
                  _const_spec((D_MODEL, D_FF)), _const_spec((D_FF, D_MODEL)), _const_spec((1, D_MODEL))],
        out_specs=row(D_MODEL),
        out_shape=jax.ShapeDtypeStruct((n, D_MODEL), F32),
        scratch_shapes=[pltpu.VMEM((tm, D_FF), BF16)],
        compiler_params=pltpu.CompilerParams(
            dimension_semantics=("arbitrary",), vmem_limit_bytes=VMEM_LIMIT),
        name="merge_out_mlp",
    )(x2, yst, ymt, yx, gates, wbs, wbm, wbx, wout, gffn, w1, w2, gfin)


def _prep_in_weights(w_in_l, b_i_l, b_f_l):
    sizes = (SWA_Q, SWA_KV, SWA_KV, MLSTM_W, MLSTM_W, MLSTM_W, MLSTM_HEADS, MLSTM_HEADS, MLSTM_W, X_W,
             N_BRANCH * D_MODEL)
    offs = np.concatenate([[0], np.cumsum(sizes)])
    sq, sk, sv, mq, mk, mv, mi, mf, mo, xq, gates = (
        w_in_l[:, int(offs[j]):int(offs[j + 1])] for j in range(len(sizes)))
    k_heads = [sk[:, hk * HEAD_DIM:(hk + 1) * HEAD_DIM] for hk in range(SWA_KV_HEADS)]
    kk = jnp.concatenate([kh for kh in k_heads for _ in range(LANES // HEAD_DIM)], axis=1)
    w_tok = jnp.concatenate([sq, kk, mq, mk, xq, gates], axis=1).astype(BF16)
    w_feat = jnp.concatenate([sv, mv, mo], axis=1).T.astype(BF16)
    w_if = jnp.concatenate([mi, mf], axis=1).T.astype(BF16)
    b_if = jnp.concatenate([b_i_l, b_f_l]).astype(F32)[:, None]
    return w_tok, w_feat, w_if, b_if


def kernel(x, mem, rel_bias, g_mix, w_in, conv_w, b_i, b_f, mlstm_norm_g, sinks, g_mem, w_mem_kv,
           w_br_swa, w_br_mlstm, w_br_x, w_out, g_ffn, w_ff1, w_ff2, g_final):
    bsz, seq, d = x.shape
    mlen = mem.shape[1]
    depth = w_in.shape[0]
    n = bsz * seq
    bias = _bias_table(rel_bias)
    x2 = x.reshape(n, d).astype(F32)
    mem2 = mem.reshape(bsz * mlen, d).astype(F32)
    row = lambda v: v.astype(F32)[None, :]
    for l in range(depth):
        w_tok, w_feat, w_if, b_if = _prep_in_weights(w_in[l], b_i[l], b_f[l])
        swa_qk, mqk, xq, gates, svt, mvt, sot, gt = _in_proj(
            x2, row(g_mix[l]), w_tok, w_feat, w_if, b_if, conv_w[l].astype(F32), seq)
        y_swa_t = _swa(swa_qk.reshape(bsz, seq, -1), svt, sinks[l], bias)
        norm_g_b = jnp.broadcast_to(mlstm_norm_g[l].astype(F32)[:, None], (MLSTM_W, LANES))
        y_ml_t = _mlstm(mqk.reshape(bsz, seq, -1), mvt, sot, gt, norm_g_b)
        mem_kv = _norm_matmul(mem2, row(g_mem[l]), w_mem_kv[l].astype(BF16))
        y_x = _xattn(xq.reshape(bsz, seq, -1), mem_kv.reshape(bsz, mlen, -1))
        x2 = _post(x2, y_swa_t, y_ml_t, y_x.reshape(n, -1), gates,
                   w_br_swa[l].astype(BF16), w_br_mlstm[l].astype(BF16), w_br_x[l].astype(BF16),
                   w_out[l].astype(BF16), row(g_ffn[l]), w_ff1[l].astype(BF16), w_ff2[l].astype(BF16),
                   row(g_final), final_norm=(l == depth - 1))
    return x2.reshape(bsz, seq, d).astype(x.dtype)
```

```python
import functools
import math

import numpy as np
import jax
import jax.numpy as jnp
from jax import lax
from jax.experimental import pallas as pl
from jax.experimental.pallas import tpu as pltpu

D_MODEL = 1024
HEAD_DIM = 64
SWA_Q_HEADS = 8
SWA_KV_HEADS = 2
SWA_GROUP = SWA_Q_HEADS // SWA_KV_HEADS
WINDOW = 128
MLSTM_HEADS = 8
MLSTM_HEAD_DIM = 64
CONV_WIDTH = 4
X_HEADS = 4
X_HEAD_DIM = 128
N_BRANCH = 3
D_FF = 4 * D_MODEL
REL_BUCKETS = 32
REL_MAX_EXACT = 16
REL_MAX_DIST = 128
EPS = 1e-6
NEG_INF = -1e30

SWA_Q = SWA_Q_HEADS * HEAD_DIM
SWA_KV = SWA_KV_HEADS * HEAD_DIM
MLSTM_W = MLSTM_HEADS * MLSTM_HEAD_DIM
X_W = X_HEADS * X_HEAD_DIM
LANES = 128

SWA_QK_W = SWA_Q + SWA_KV_HEADS * LANES
C_SWA = 0
C_MQK = C_SWA + SWA_QK_W
C_XQ = C_MQK + 2 * MLSTM_W
C_GATE = C_XQ + X_W
C_END = C_GATE + N_BRANCH * D_MODEL
R_SV = 0
R_MV = R_SV + SWA_KV
R_MO = R_MV + MLSTM_W
R_END = R_MO + MLSTM_W
N_IF = 2 * MLSTM_HEADS

MLSTM_CHUNK = 128
STATE_ROWS = MLSTM_HEAD_DIM + 16
CONV_HALO = 8

VMEM_LIMIT = 56 * 1024 * 1024

BF16 = jnp.bfloat16
F32 = jnp.float32


def _rms(x, g):
    return x * lax.rsqrt(jnp.mean(x * x, axis=-1, keepdims=True) + EPS) * g


def _const_spec(shape):
    nd = len(shape)
    return pl.BlockSpec(shape, lambda *_: (0,) * nd, pipeline_mode=pl.Buffered(1))


def _dot(a, b):
    return jnp.dot(a, b, preferred_element_type=F32)


def _dot_nt(a, b):
    return lax.dot_general(a, b, (((1,), (1,)), ((), ())), preferred_element_type=F32)


def _dot_tn(a, b):
    return lax.dot_general(a, b, (((0,), (0,)), ((), ())), preferred_element_type=F32)


def _in_proj_kernel(x_ref, g_ref, w_ref, wt_ref, wift_ref, bift_ref, convw_ref,
                    swa_ref, mqk_ref, xq_ref, gate_ref, svt_ref, mvt_ref, sot_ref, gt_ref,
                    conv_buf, *, blocks_per_seq):
    i = pl.program_id(0)
    tm = x_ref.shape[0]

    @pl.when(i == 0)
    def _():
        conv_buf[tm:tm + CONV_HALO, :] = jnp.zeros((CONV_HALO, 2 * MLSTM_W), F32)

    h = _rms(x_ref[...], g_ref[...]).astype(BF16)

    def proj(lo, width):
        return _dot(h, w_ref[:, lo:lo + width])

    def proj_t(lo, rows):
        return _dot_nt(wt_ref[lo:lo + rows, :], h)

    prev_tail = conv_buf[tm:tm + CONV_HALO, :]
    conv_buf[0:CONV_HALO, :] = jnp.where(i % blocks_per_seq == 0, 0.0, prev_tail)
    conv_buf[CONV_HALO:CONV_HALO + tm, :] = proj(C_MQK, 2 * MLSTM_W)
    for part, scale in ((0, 1.0), (1, MLSTM_HEAD_DIM ** -0.5)):
        cols = slice(part * MLSTM_W, (part + 1) * MLSTM_W)
        acc = None
        for j in range(CONV_WIDTH):
            r0 = CONV_HALO - (CONV_WIDTH - 1) + j
            term = convw_ref[j:j + 1, cols] * conv_buf[r0:r0 + tm, cols]
            acc = term if acc is None else acc + term
        act = acc * jax.nn.sigmoid(acc)
        mqk_ref[:, cols] = (act * scale).astype(BF16)

    sot_ref[...] = jax.nn.sigmoid(proj_t(R_MO, MLSTM_W)).astype(BF16)
    for k in range(N_BRANCH):
        gate_ref[:, k * D_MODEL:(k + 1) * D_MODEL] = jax.nn.sigmoid(
            proj(C_GATE + k * D_MODEL, D_MODEL)).astype(BF16)
    gt_ref[...] = _dot_nt(wift_ref[...], h) + bift_ref[...]
    svt_ref[...] = proj_t(R_SV, SWA_KV).astype(BF16)
    mvt_ref[...] = proj_t(R_MV, MLSTM_W).astype(BF16)
    swa_ref[...] = proj(C_SWA, SWA_QK_W).astype(BF16)
    xq_ref[...] = proj(C_XQ, X_W).astype(BF16)


def _in_proj(x2, g, w, wt, wift, bift, convw, seq_len, tm=512):
    n = x2.shape[0]
    assert n % tm == 0 and seq_len % tm == 0
    row = lambda width: pl.BlockSpec((tm, width), lambda i: (i, 0))
    col = lambda rows: pl.BlockSpec((rows, tm), lambda i: (0, i))
    tok_widths = (SWA_QK_W, 2 * MLSTM_W, X_W, N_BRANCH * D_MODEL)
    feat_rows = (SWA_KV, MLSTM_W, MLSTM_W)
    out_shape = ([jax.ShapeDtypeStruct((n, wd), BF16) for wd in tok_widths]
                 + [jax.ShapeDtypeStruct((r, n), BF16) for r in feat_rows]
                 + [jax.ShapeDtypeStruct((N_IF, n), F32)])
    return pl.pallas_call(
        functools.partial(_in_proj_kernel, blocks_per_seq=seq_len // tm),
        grid=(n // tm,),
        in_specs=[row(D_MODEL), _const_spec((1, D_MODEL)), _const_spec((D_MODEL, C_END)),
                  _const_spec((R_END, D_MODEL)), _const_spec((N_IF, D_MODEL)), _const_spec((N_IF, 1)),
                  _const_spec((CONV_WIDTH, 2 * MLSTM_W))],
        out_specs=[row(wd) for wd in tok_widths] + [col(r) for r in feat_rows] + [col(N_IF)],
        out_shape=out_shape,
        scratch_shapes=[pltpu.VMEM((tm + CONV_HALO, 2 * MLSTM_W), F32)],
        compiler_params=pltpu.CompilerParams(
            dimension_semantics=("arbitrary",), vmem_limit_bytes=VMEM_LIMIT),
        name="in_proj",
    )(x2, g, w, wt, wift, bift, convw)


def _t5_bucket_table_t():
    qi = np.arange(WINDOW)[:, None]
    kj = np.arange(2 * WINDOW)[None, :]
    dist = qi + WINDOW - kj
    n = np.maximum(dist, 0)
    nf = np.maximum(n, 1).astype(np.float32)
    scale = np.float32((REL_BUCKETS - REL_MAX_EXACT) / math.log(REL_MAX_DIST / REL_MAX_EXACT))
    large = REL_MAX_EXACT + (np.log(nf / np.float32(REL_MAX_EXACT)) * scale).astype(np.int32)
    large = np.minimum(large, REL_BUCKETS - 1)
    bucket = np.where(n < REL_MAX_EXACT, n, large)
    band = (dist >= 0) & (dist < WINDOW)
    return np.where(band, bucket, -1).astype(np.int32).T.copy()


def _bias_kernel(rel_ref, bucket_ref, o_ref):
    bucket = bucket_ref[...]
    for h in range(SWA_Q_HEADS):
        acc = jnp.full(bucket.shape, NEG_INF, F32)
        for b in range(REL_BUCKETS):
            acc = jnp.where(bucket == b, rel_ref[b, h], acc)
        hk, g = divmod(h, SWA_GROUP)
        o_ref[hk, :, g * WINDOW:(g + 1) * WINDOW] = acc


def _bias_table(rel_bias):
    bucket = jnp.asarray(_t5_bucket_table_t())
    shape = (SWA_KV_HEADS, 2 * WINDOW, SWA_GROUP * WINDOW)
    return pl.pallas_call(
        _bias_kernel,
        in_specs=[pl.BlockSpec(memory_space=pltpu.SMEM),
                  pl.BlockSpec((2 * WINDOW, WINDOW), lambda: (0, 0))],
        out_specs=pl.BlockSpec(shape, lambda: (0, 0, 0)),
        out_shape=jax.ShapeDtypeStruct(shape, F32),
        name="t5_bias_table",
    )(rel_bias.astype(F32), bucket)


def _swa_kernel(sinks_ref, qk_ref, kprev_ref, vt_ref, vtprev_ref, bias_ref, o_ref):
    n = pl.program_id(1)
    tq = qk_ref.shape[0]
    cols = SWA_GROUP * WINDOW
    kj = lax.broadcasted_iota(jnp.int32, (2 * WINDOW, cols), 0)
    lo_half = lax.broadcasted_iota(jnp.int32, (WINDOW, LANES), 1) < HEAD_DIM
    for blk in range(tq // WINDOW):
        r0 = blk * WINDOW
        for hk in range(SWA_KV_HEADS):
            kc = slice(SWA_Q + hk * LANES, SWA_Q + (hk + 1) * LANES)
            vr = slice(hk * HEAD_DIM, (hk + 1) * HEAD_DIM)
            if blk == 0:
                k2 = jnp.concatenate([kprev_ref[:, hk * LANES:(hk + 1) * LANES], qk_ref[0:WINDOW, kc]], axis=0)
                v2t = jnp.concatenate([vtprev_ref[vr, :], vt_ref[vr, 0:WINDOW]], axis=1)
            else:
                k2 = qk_ref[r0 - WINDOW:r0 + WINDOW, kc]
                v2t = vt_ref[vr, r0 - WINDOW:r0 + WINDOW]
            tiles = []
            for pair in (2 * hk, 2 * hk + 1):
                t = qk_ref[r0:r0 + WINDOW, pair * LANES:(pair + 1) * LANES]
                tiles += [jnp.where(lo_half, t, 0), jnp.where(lo_half, 0, t)]
            q4 = jnp.concatenate(tiles, axis=0)
            s = _dot_nt(k2, q4) * HEAD_DIM ** -0.5 + bias_ref[hk]
            if blk == 0:
                s = jnp.where((kj >= WINDOW) | (n > 0), s, NEG_INF)
            heads = [hk * SWA_GROUP + g for g in range(SWA_GROUP)]
            sink = jnp.concatenate([jnp.full((1, WINDOW), sinks_ref[hd], F32) for hd in heads], axis=1)
            m = jnp.maximum(jnp.max(s, axis=0, keepdims=True), sink)
            p = jnp.exp(s - m)
            denom = jnp.sum(p, axis=0, keepdims=True) + jnp.exp(sink - m)
            o = _dot(v2t, p.astype(BF16)) / denom
            for g, hd in enumerate(heads):
                o_ref[hd * HEAD_DIM:(hd + 1) * HEAD_DIM, r0:r0 + WINDOW] = (
                    o[:, g * WINDOW:(g + 1) * WINDOW].astype(BF16))


def _swa(swa_qk3, svt, sinks, bias, tq=512):
    b, t, wd = swa_qk3.shape
    nblk = tq // WINDOW
    steps = t // tq
    return pl.pallas_call(
        _swa_kernel,
        grid=(b, steps),
        in_specs=[pl.BlockSpec(memory_space=pltpu.SMEM),
                  pl.BlockSpec((None, tq, wd), lambda bi, n: (bi, n, 0)),
                  pl.BlockSpec((None, WINDOW, SWA_KV_HEADS * LANES),
                               lambda bi, n: (bi, jnp.maximum(n * nblk - 1, 0), SWA_Q // (SWA_KV_HEADS * LANES))),
                  pl.BlockSpec((SWA_KV, tq), lambda bi, n: (0, bi * steps + n)),
                  pl.BlockSpec((SWA_KV, WINDOW), lambda bi, n: (0, jnp.maximum((bi * steps + n) * nblk - 1, 0))),
                  _const_spec((SWA_KV_HEADS, 2 * WINDOW, SWA_GROUP * WINDOW))],
        out_specs=pl.BlockSpec((SWA_Q, tq), lambda bi, n: (0, bi * steps + n)),
        out_shape=jax.ShapeDtypeStruct((SWA_Q, b * t), BF16),
        compiler_params=pltpu.CompilerParams(dimension_semantics=("arbitrary", "arbitrary")),
        name="swa_attention",
    )(sinks.astype(F32), swa_qk3, swa_qk3, svt, svt, bias)


def _split3(x):
    hi = x.astype(BF16).astype(F32)
    r = x - hi
    mid = r.astype(BF16).astype(F32)
    lo = (r - mid).astype(BF16).astype(F32)
    return [hi, mid, lo]


def _log_sigmoid(x):
    return jnp.minimum(x, 0.0) - jnp.log1p(jnp.exp(-jnp.abs(x)))


def _mlstm_kernel(qk_ref, vt_ref, sot_ref, gt_ref, ngb_ref, o_ref, state_ref, m_ref):
    L = MLSTM_CHUNK
    H, Dh = MLSTM_HEADS, MLSTM_HEAD_DIM
    SR = STATE_ROWS
    nc = qk_ref.shape[0] // L

    @pl.when(pl.program_id(1) == 0)
    def _():
        state_ref[...] = jnp.zeros(state_ref.shape, F32)
        m_ref[...] = jnp.zeros(m_ref.shape, F32)

    s_i = lax.broadcasted_iota(jnp.int32, (L, 2 * L), 0)
    t_i = lax.broadcasted_iota(jnp.int32, (L, 2 * L), 1) % L
    causal2 = s_i <= t_i
    tri_u = causal2[:, 0:L].astype(BF16)
    lo_half = lax.broadcasted_iota(jnp.int32, (L, LANES), 1) < Dh
    n_row = lax.broadcasted_iota(jnp.int32, (SR - Dh, L), 0) == 0
    block_sel = (lax.broadcasted_iota(jnp.int32, (6 * H, H * L), 0) % H
                 == lax.broadcasted_iota(jnp.int32, (6 * H, H * L), 1) // L)

    def stack(lo):
        return jnp.concatenate([gt_ref[lo:lo + H, c * L:(c + 1) * L] for c in range(nc)], axis=0)

    ig = stack(0)
    lf = _log_sigmoid(stack(H))
    a = sum(_dot(part.astype(BF16), tri_u) for part in _split3(lf))
    u = ig - a
    lane = lax.broadcasted_iota(jnp.int32, u.shape, 1)
    cm = u
    sh = 1
    while sh < L:
        cm = jnp.maximum(cm, jnp.where(lane >= sh, pltpu.roll(cm, sh, 1), NEG_INF))
        sh *= 2
    umax = cm[:, L - 1:L]
    g = a[:, L - 1:L]
    m = m_ref[:, 0:1]
    m_list = [m]
    for c in range(nc):
        gc = g[c * H:(c + 1) * H]
        m = jnp.maximum(gc + m, gc + umax[c * H:(c + 1) * H])
        m_list.append(m)
    m_ref[...] = jnp.broadcast_to(m, m_ref.shape)
    m_in = jnp.concatenate(m_list[:-1], axis=0)
    m_out = jnp.concatenate(m_list[1:], axis=0)
    big_m = jnp.maximum(cm, m_in)
    w = jnp.exp(u - umax)
    inter_w = jnp.exp(m_in - big_m)
    floor = jnp.exp(-(a + big_m))
    decay = jnp.exp(g + m_in - m_out)
    inject = jnp.exp(g + umax - m_out)
    u3 = _split3(u)
    nm3 = _split3(-big_m)
    ones_rows = jnp.ones((3 * H, L), F32)
    ones_blk = jnp.ones((3 * H, H * L), F32)

    def col(vec, r):
        return jnp.broadcast_to(vec[r:r + 1, :], (SR, 1))

    for c in range(nc):
        rows = slice(c * L, (c + 1) * L)
        g0 = c * H
        at = jnp.concatenate([part[g0:g0 + H] for part in u3] + [ones_rows], axis=0)
        nm = jnp.concatenate([part[g0:g0 + H] for part in nm3], axis=0)
        bb = jnp.concatenate([ones_blk, jnp.concatenate([nm] * H, axis=1)], axis=0)
        e = _dot_tn(at.astype(BF16), jnp.where(block_sel, bb, 0.0).astype(BF16))

        work = []
        for pair in range(H // 2):
            pr = slice(pair * LANES, (pair + 1) * LANES)
            qp = qk_ref[rows, pr]
            kp = qk_ref[rows, MLSTM_W + pair * LANES:MLSTM_W + (pair + 1) * LANES]
            qz2 = jnp.concatenate([jnp.where(lo_half, qp, 0), jnp.where(lo_half, 0, qp)], axis=0)
            vt2 = vt_ref[pr, rows]
            d2 = jnp.exp(jnp.where(causal2, e[:, pair * 2 * L:(pair + 1) * 2 * L], NEG_INF))
            p2 = _dot_nt(kp, qz2) * d2
            den2 = jnp.sum(p2, axis=0, keepdims=True)
            num2 = _dot(vt2, p2.astype(BF16))
            vt2f = vt2.astype(F32)
            aug = []
            for j in range(2):
                w_r = w[g0 + 2 * pair + j:g0 + 2 * pair + j + 1]
                aug += [vt2f[j * Dh:(j + 1) * Dh] * w_r, jnp.where(n_row, w_r, 0.0)]
            d_state = _dot(jnp.concatenate(aug, axis=0).astype(BF16), kp)
            work.append((qz2, den2, num2, d_state))

        for pair in range(H // 2):
            qz2, den2, num2, d_state = work[pair]
            state = state_ref[pair]
            inter2 = _dot_nt(state.astype(BF16), qz2)
            for j in range(2):
                h = 2 * pair + j
                r = g0 + h
                hr = slice(h * Dh, (h + 1) * Dh)
                tl = slice(j * L, (j + 1) * L)
                iw = inter_w[r:r + 1]
                num = num2[j * Dh:(j + 1) * Dh, tl] + iw * inter2[j * SR:j * SR + Dh, tl]
                den = den2[:, tl] + iw * inter2[j * SR + Dh:j * SR + Dh + 1, tl]
                hh = num / jnp.maximum(jnp.abs(den), floor[r:r + 1])
                mu = jnp.mean(hh, axis=0, keepdims=True)
                ctr = hh - mu
                var = jnp.mean(ctr * ctr, axis=0, keepdims=True)
                y = sot_ref[hr, rows].astype(F32) * (ctr * lax.rsqrt(var + EPS) * ngb_ref[hr, :])
                o_ref[hr, rows] = y.astype(BF16)
            dec = jnp.concatenate([col(decay, g0 + 2 * pair), col(decay, g0 + 2 * pair + 1)], axis=0)
            inj = jnp.concatenate([col(inject, g0 + 2 * pair), col(inject, g0 + 2 * pair + 1)], axis=0)
            state_ref[pair] = dec * state + inj * d_state


def _mlstm(qk3, mvt, sot, gt, norm_g_b, ts=512):
    b, t, _ = qk3.shape
    steps = t // ts
    col = lambda rows: pl.BlockSpec((rows, ts), lambda bi, j: (0, bi * steps + j))
    return pl.pallas_call(
        _mlstm_kernel,
        grid=(b, steps),
        in_specs=[pl.BlockSpec((None, ts, 2 * MLSTM_W), lambda bi, j: (bi, j, 0)),
                  col(MLSTM_W), col(MLSTM_W), col(N_IF), _const_spec((MLSTM_W, LANES))],
        out_specs=col(MLSTM_W),
        out_shape=jax.ShapeDtypeStruct((MLSTM_W, b * t), BF16),
        scratch_shapes=[pltpu.VMEM((MLSTM_HEADS // 2, 2 * STATE_ROWS, LANES), F32),
                        pltpu.VMEM((MLSTM_HEADS, LANES), F32)],
        compiler_params=pltpu.CompilerParams(dimension_semantics=("arbitrary", "arbitrary")),
        name="mlstm",
    )(qk3, mvt, sot, gt, norm_g_b)


def _norm_matmul_kernel(x_ref, g_ref, w_ref, o_ref):
    o_ref[...] = _dot(_rms(x_ref[...], g_ref[...]).astype(BF16), w_ref[...]).astype(o_ref.dtype)


def _norm_matmul(x2, g, w, tm=512):
    n, d = x2.shape
    nout = w.shape[1]
    return pl.pallas_call(
        _norm_matmul_kernel,
        grid=(n // tm,),
        in_specs=[pl.BlockSpec((tm, d), lambda i: (i, 0)), _const_spec((1, d)), _const_spec((d, nout))],
        out_specs=pl.BlockSpec((tm, nout), lambda i: (i, 0)),
        out_shape=jax.ShapeDtypeStruct((n, nout), BF16),
        compiler_params=pltpu.CompilerParams(dimension_semantics=("arbitrary",)),
        name="mem_kv_proj",
    )(x2, g, w)


def _xattn_kernel(q_ref, kv_ref, o_ref):
    for h in range(X_HEADS):
        hc = slice(h * X_HEAD_DIM, (h + 1) * X_HEAD_DIM)
        s = _dot_nt(q_ref[:, hc], kv_ref[:, hc]) * X_HEAD_DIM ** -0.5
        p = jnp.exp(s - jnp.max(s, axis=-1, keepdims=True))
        denom = jnp.sum(p, axis=-1, keepdims=True)
        o = _dot(p.astype(BF16), kv_ref[:, X_W + h * X_HEAD_DIM:X_W + (h + 1) * X_HEAD_DIM]) / denom
        o_ref[:, hc] = o.astype(BF16)


def _xattn(xq3, kv3, tq=512):
    b, t, _ = xq3.shape
    m = kv3.shape[1]
    return pl.pallas_call(
        _xattn_kernel,
        grid=(b, t // tq),
        in_specs=[pl.BlockSpec((None, tq, X_W), lambda bi, j: (bi, j, 0)),
                  pl.BlockSpec((None, m, 2 * X_W), lambda bi, j: (bi, 0, 0))],
        out_specs=pl.BlockSpec((None, tq, X_W), lambda bi, j: (bi, j, 0)),
        out_shape=jax.ShapeDtypeStruct((b, t, X_W), BF16),
        compiler_params=pltpu.CompilerParams(dimension_semantics=("arbitrary", "arbitrary")),
        name="cross_attention",
    )(xq3, kv3)


def _post_kernel(x_ref, yst_ref, ymt_ref, yx_ref, gate_ref, wbs_ref, wbm_ref, wbx_ref, wout_ref,
                 gffn_ref, w1_ref, w2_ref, gfin_ref, o_ref, hid_ref, *, final_norm):
    def gate(k):
        return gate_ref[:, k * D_MODEL:(k + 1) * D_MODEL].astype(F32)

    merged = (gate(0) * _dot_tn(yst_ref[...], wbs_ref[...])
              + gate(1) * _dot_tn(ymt_ref[...], wbm_ref[...])
              + gate(2) * _dot(yx_ref[...], wbx_ref[...]))
    x1 = x_ref[...] + _dot(merged.astype(BF16), wout_ref[...])
    h = _rms(x1, gffn_ref[...]).astype(BF16)
    step = D_MODEL
    for c in range(D_FF // step):
        a = jnp.maximum(_dot(h, w1_ref[:, c * step:(c + 1) * step]), 0.0)
        hid_ref[:, c * step:(c + 1) * step] = (a * a).astype(BF16)
    x2 = x1 + _dot(hid_ref[...], w2_ref[...])
    if final_norm:
        x2 = _rms(x2, gfin_ref[...])
    o_ref[...] = x2


def _post(x2, yst, ymt, yx, gates, wbs, wbm, wbx, wout, gffn, w1, w2, gfin, final_norm, tm=256):
    n = x2.shape[0]
    row = lambda wd: pl.BlockSpec((tm, wd), lambda i: (i, 0))
    col = lambda rows: pl.BlockSpec((rows, tm), lambda i: (0, i))
    return pl.pallas_call(
        functools.partial(_post_kernel, final_norm=final_norm),
        grid=(n // tm,),
        in_specs=[row(D_MODEL), col(SWA_Q), col(MLSTM_W), row(X_W), row(N_BRANCH * D_MODEL),
                  _const_spec((SWA_Q, D_MODEL)), _const_spec((MLSTM_W, D_MODEL)), _const_spec((X_W, D_MODEL)),
                  _const_spec((D_MODEL, D_MODEL)), _const_spec((1, D_MODEL)),
                  _const_spec((D_MODEL, D_FF)), _const_spec((D_FF, D_MODEL)), _const_spec((1, D_MODEL))],
        out_specs=row(D_MODEL),
        out_shape=jax.ShapeDtypeStruct((n, D_MODEL), F32),
        scratch_shapes=[pltpu.VMEM((tm, D_FF), BF16)],
        compiler_params=pltpu.CompilerParams(
            dimension_semantics=("arbitrary",), vmem_limit_bytes=VMEM_LIMIT),
        name="merge_out_mlp",
    )(x2, yst, ymt, yx, gates, wbs, wbm, wbx, wout, gffn, w1, w2, gfin)


def _prep_in_weights(w_in_l, b_i_l, b_f_l):
    sizes = (SWA_Q, SWA_KV, SWA_KV, MLSTM_W, MLSTM_W, MLSTM_W, MLSTM_HEADS, MLSTM_HEADS, MLSTM_W, X_W,
             N_BRANCH * D_MODEL)
    offs = np.concatenate([[0], np.cumsum(sizes)])
    sq, sk, sv, mq, mk, mv, mi, mf, mo, xq, gates = (
        w_in_l[:, int(offs[j]):int(offs[j + 1])] for j in range(len(sizes)))
    k_heads = [sk[:, hk * HEAD_DIM:(hk + 1) * HEAD_DIM] for hk in range(SWA_KV_HEADS)]
    kk = jnp.concatenate([kh for kh in k_heads for _ in range(LANES // HEAD_DIM)], axis=1)
    w_tok = jnp.concatenate([sq, kk, mq, mk, xq, gates], axis=1).astype(BF16)
    w_feat = jnp.concatenate([sv, mv, mo], axis=1).T.astype(BF16)
    w_if = jnp.concatenate([mi, mf], axis=1).T.astype(BF16)
    b_if = jnp.concatenate([b_i_l, b_f_l]).astype(F32)[:, None]
    return w_tok, w_feat, w_if, b_if


def kernel(x, mem, rel_bias, g_mix, w_in, conv_w, b_i, b_f, mlstm_norm_g, sinks, g_mem, w_mem_kv,
           w_br_swa, w_br_mlstm, w_br_x, w_out, g_ffn, w_ff1, w_ff2, g_final):
    bsz, seq, d = x.shape
    mlen = mem.shape[1]
    depth = w_in.shape[0]
    n = bsz * seq
    bias = _bias_table(rel_bias)
    x2 = x.reshape(n, d).astype(F32)
    mem2 = mem.reshape(bsz * mlen, d).astype(F32)
    row = lambda v: v.astype(F32)[None, :]
    for l in range(depth):
        w_tok, w_feat, w_if, b_if = _prep_in_weights(w_in[l], b_i[l], b_f[l])
        swa_qk, mqk, xq, gates, svt, mvt, sot, gt = _in_proj(
            x2, row(g_mix[l]), w_tok, w_feat, w_if, b_if, conv_w[l].astype(F32), seq)
        y_swa_t = _swa(swa_qk.reshape(bsz, seq, -1), svt, sinks[l], bias)
        norm_g_b = jnp.broadcast_to(mlstm_norm_g[l].astype(F32)[:, None], (MLSTM_W, LANES))
        y_ml_t = _mlstm(mqk.reshape(bsz, seq, -1), mvt, sot, gt, norm_g_b)
        mem_kv = _norm_matmul(mem2, row(g_mem[l]), w_mem_kv[l].astype(BF16))
        y_x = _xattn(xq.reshape(bsz, seq, -1), mem_kv.reshape(bsz, mlen, -1))
        x2 = _post(x2, y_swa_t, y_ml_t, y_x.reshape(n, -1), gates,
                   w_br_swa[l].astype(BF16), w_br_mlstm[l].astype(BF16), w_br_x[l].astype(BF16),
                   w_out[l].astype(BF16), row(g_ffn[l]), w_ff1[l].astype(BF16), w_ff2[l].astype(BF16),
                   row(g_final), final_norm=(l == depth - 1))
    return x2.reshape(bsz, seq, d).astype(x.dtype)
```

```python
import functools
import math

import numpy as np
import jax
import jax.numpy as jnp
from jax import lax
from jax.experimental import pallas as pl
from jax.experimental.pallas import tpu as pltpu

D_MODEL = 1024
HEAD_DIM = 64
SWA_Q_HEADS = 8
SWA_KV_HEADS = 2
SWA_GROUP = SWA_Q_HEADS // SWA_KV_HEADS
WINDOW = 128
MLSTM_HEADS = 8
MLSTM_HEAD_DIM = 64
CONV_WIDTH = 4
X_HEADS = 4
X_HEAD_DIM = 128
N_BRANCH = 3
D_FF = 4 * D_MODEL
REL_BUCKETS = 32
REL_MAX_EXACT = 16
REL_MAX_DIST = 128
EPS = 1e-6
NEG_INF = -1e30

SWA_Q = SWA_Q_HEADS * HEAD_DIM
SWA_KV = SWA_KV_HEADS * HEAD_DIM
MLSTM_W = MLSTM_HEADS * MLSTM_HEAD_DIM
X_W = X_HEADS * X_HEAD_DIM
LANES = 128

C_MQK = 0
C_SK = C_MQK + 2 * MLSTM_W
C_XQ = C_SK + SWA_KV
C_GATE = C_XQ + X_W
C_END = C_GATE + N_BRANCH * D_MODEL
N_IF = 2 * MLSTM_HEADS
R_SQ = 0
R_SV = R_SQ + SWA_Q
R_MV = R_SV + SWA_KV
R_MO = R_MV + MLSTM_W
R_IF = R_MO + MLSTM_W
R_END = R_IF + N_IF
LOG2E = math.log2(math.e)

MLSTM_CHUNK = 128
STATE_ROWS = MLSTM_HEAD_DIM + 16
CONV_HALO = 8

VMEM_LIMIT = 56 * 1024 * 1024

BF16 = jnp.bfloat16
F32 = jnp.float32


def _rms(x, g):
    return x * lax.rsqrt(jnp.mean(x * x, axis=-1, keepdims=True) + EPS) * g


def _const_spec(shape):
    nd = len(shape)
    return pl.BlockSpec(shape, lambda *_: (0,) * nd, pipeline_mode=pl.Buffered(1))


def _dot(a, b):
    return jnp.dot(a, b, preferred_element_type=F32)


def _dot_nt(a, b):
    return lax.dot_general(a, b, (((1,), (1,)), ((), ())), preferred_element_type=F32)


def _dot_tn(a, b):
    return lax.dot_general(a, b, (((0,), (0,)), ((), ())), preferred_element_type=F32)


def _in_proj_kernel(x_ref, g_ref, w_ref, wt_ref, bift_ref, convw_ref,
                    mqk_ref, sk_ref, xq_ref, gate_ref, sqt_ref, svt_ref, mvt_ref, sot_ref, gt_ref,
                    conv_buf, *, blocks_per_seq):
    i = pl.program_id(0)
    tm = x_ref.shape[0]

    @pl.when(i == 0)
    def _():
        conv_buf[tm:tm + CONV_HALO, :] = jnp.zeros((CONV_HALO, 2 * MLSTM_W), F32)

    h = _rms(x_ref[...], g_ref[...]).astype(BF16)

    def proj(lo, width):
        return _dot(h, w_ref[:, lo:lo + width])

    prev_tail = conv_buf[tm:tm + CONV_HALO, :]
    conv_buf[0:CONV_HALO, :] = jnp.where(i % blocks_per_seq == 0, 0.0, prev_tail)
    conv_buf[CONV_HALO:CONV_HALO + tm, :] = proj(C_MQK, 2 * MLSTM_W)
    for part, scale in ((0, 1.0), (1, MLSTM_HEAD_DIM ** -0.5)):
        cols = slice(part * MLSTM_W, (part + 1) * MLSTM_W)
        acc = None
        for j in range(CONV_WIDTH):
            r0 = CONV_HALO - (CONV_WIDTH - 1) + j
            term = convw_ref[j:j + 1, cols] * conv_buf[r0:r0 + tm, cols]
            acc = term if acc is None else acc + term
        act = acc * jax.nn.sigmoid(acc)
        mqk_ref[:, cols] = (act * scale).astype(BF16)

    feat = _dot_nt(wt_ref[...], h)
    sot_ref[...] = jax.nn.sigmoid(feat[R_MO:R_IF]).astype(BF16)
    gt_ref[...] = feat[R_IF:R_END] + bift_ref[...]
    sqt_ref[...] = feat[R_SQ:R_SV].astype(BF16)
    svt_ref[...] = feat[R_SV:R_MV].astype(BF16)
    mvt_ref[...] = feat[R_MV:R_MO].astype(BF16)
    for k in range(N_BRANCH):
        gate_ref[:, k * D_MODEL:(k + 1) * D_MODEL] = jax.nn.sigmoid(
            proj(C_GATE + k * D_MODEL, D_MODEL)).astype(BF16)
    skxq = proj(C_SK, SWA_KV + X_W)
    sk_ref[...] = skxq[:, 0:SWA_KV].astype(BF16)
    xq_ref[...] = skxq[:, SWA_KV:].astype(BF16)


def _in_proj(x2, g, w, wt, bift, convw, seq_len, tm=512):
    n = x2.shape[0]
    assert n % tm == 0 and seq_len % tm == 0
    row = lambda width: pl.BlockSpec((tm, width), lambda i: (i, 0))
    col = lambda rows: pl.BlockSpec((rows, tm), lambda i: (0, i))
    tok_widths = (2 * MLSTM_W, SWA_KV, X_W, N_BRANCH * D_MODEL)
    feat_rows = (SWA_Q, SWA_KV, MLSTM_W, MLSTM_W)
    out_shape = ([jax.ShapeDtypeStruct((n, wd), BF16) for wd in tok_widths]
                 + [jax.ShapeDtypeStruct((r, n), BF16) for r in feat_rows]
                 + [jax.ShapeDtypeStruct((N_IF, n), F32)])
    return pl.pallas_call(
        functools.partial(_in_proj_kernel, blocks_per_seq=seq_len // tm),
        grid=(n // tm,),
        in_specs=[row(D_MODEL), _const_spec((1, D_MODEL)), _const_spec((D_MODEL, C_END)),
                  _const_spec((R_END, D_MODEL)), _const_spec((N_IF, 1)),
                  _const_spec((CONV_WIDTH, 2 * MLSTM_W))],
        out_specs=[row(wd) for wd in tok_widths] + [col(r) for r in feat_rows] + [col(N_IF)],
        out_shape=out_shape,
        scratch_shapes=[pltpu.VMEM((tm + CONV_HALO, 2 * MLSTM_W), F32)],
        compiler_params=pltpu.CompilerParams(
            dimension_semantics=("arbitrary",), vmem_limit_bytes=VMEM_LIMIT),
        name="in_proj",
    )(x2, g, w, wt, bift, convw)


def _t5_bucket_table_t():
    qi = np.arange(WINDOW)[:, None]
    kj = np.arange(2 * WINDOW)[None, :]
    dist = qi + WINDOW - kj
    n = np.maximum(dist, 0)
    nf = np.maximum(n, 1).astype(np.float32)
    scale = np.float32((REL_BUCKETS - REL_MAX_EXACT) / math.log(REL_MAX_DIST / REL_MAX_EXACT))
    large = REL_MAX_EXACT + (np.log(nf / np.float32(REL_MAX_EXACT)) * scale).astype(np.int32)
    large = np.minimum(large, REL_BUCKETS - 1)
    bucket = np.where(n < REL_MAX_EXACT, n, large)
    band = (dist >= 0) & (dist < WINDOW)
    return np.where(band, bucket, -1).astype(np.int32).T.copy()


def _bias_kernel(rel_ref, bucket_ref, o_ref):
    bucket = bucket_ref[...]
    for h in range(SWA_Q_HEADS):
        acc = jnp.full(bucket.shape, NEG_INF, F32)
        for b in range(REL_BUCKETS):
            acc = jnp.where(bucket == b, rel_ref[b, h] * LOG2E, acc)
        hk, g = divmod(h, SWA_GROUP)
        o_ref[hk, :, g * WINDOW:(g + 1) * WINDOW] = acc


def _bias_table(rel_bias):
    bucket = jnp.asarray(_t5_bucket_table_t())
    shape = (SWA_KV_HEADS, 2 * WINDOW, SWA_GROUP * WINDOW)
    return pl.pallas_call(
        _bias_kernel,
        in_specs=[pl.BlockSpec(memory_space=pltpu.SMEM),
                  pl.BlockSpec((2 * WINDOW, WINDOW), lambda: (0, 0))],
        out_specs=pl.BlockSpec(shape, lambda: (0, 0, 0)),
        out_shape=jax.ShapeDtypeStruct(shape, F32),
        name="t5_bias_table",
    )(rel_bias.astype(F32), bucket)


def _swa_kernel(sinks_ref, qt_ref, k_ref, kprev_ref, vt_ref, vtprev_ref, bias_ref, o_ref):
    n = pl.program_id(1)
    tq = k_ref.shape[0]
    cols = SWA_GROUP * WINDOW
    kj = lax.broadcasted_iota(jnp.int32, (2 * WINDOW, cols), 0)
    zeros = jnp.zeros((HEAD_DIM, cols), BF16)
    for blk in range(tq // WINDOW):
        r0 = blk * WINDOW
        for hk in range(SWA_KV_HEADS):
            vr = slice(hk * HEAD_DIM, (hk + 1) * HEAD_DIM)
            if blk == 0:
                k2 = jnp.concatenate([kprev_ref[...], k_ref[0:WINDOW, :]], axis=0)
                v2t = jnp.concatenate([vtprev_ref[vr, :], vt_ref[vr, 0:WINDOW]], axis=1)
            else:
                k2 = k_ref[r0 - WINDOW:r0 + WINDOW, :]
                v2t = vt_ref[vr, r0 - WINDOW:r0 + WINDOW]
            heads = [hk * SWA_GROUP + g for g in range(SWA_GROUP)]
            q4t = jnp.concatenate(
                [qt_ref[hd * HEAD_DIM:(hd + 1) * HEAD_DIM, r0:r0 + WINDOW] for hd in heads], axis=1)
            rhs = jnp.concatenate([q4t, zeros] if hk == 0 else [zeros, q4t], axis=0)
            s = _dot(k2, rhs) * LOG2E + bias_ref[hk]
            if blk == 0:
                s = jnp.where((kj >= WINDOW) | (n > 0), s, NEG_INF)
            sink = jnp.concatenate(
                [jnp.full((1, WINDOW), sinks_ref[hd] * LOG2E, F32) for hd in heads], axis=1)
            m = jnp.maximum(jnp.max(s, axis=0, keepdims=True), sink)
            p = jnp.exp2(s - m)
            denom = jnp.sum(p, axis=0, keepdims=True) + jnp.exp2(sink - m)
            o = _dot(v2t, p.astype(BF16)) * (1.0 / denom)
            for g, hd in enumerate(heads):
                o_ref[hd * HEAD_DIM:(hd + 1) * HEAD_DIM, r0:r0 + WINDOW] = (
                    o[:, g * WINDOW:(g + 1) * WINDOW].astype(BF16))


def _swa(sqt, sk3, svt, sinks, bias, tq=512):
    b, t, wd = sk3.shape
    nblk = tq // WINDOW
    steps = t // tq
    prev_col = lambda bi, n: jnp.maximum((bi * steps + n) * nblk - 1, 0)
    return pl.pallas_call(
        _swa_kernel,
        grid=(b, steps),
        in_specs=[pl.BlockSpec(memory_space=pltpu.SMEM),
                  pl.BlockSpec((SWA_Q, tq), lambda bi, n: (0, bi * steps + n)),
                  pl.BlockSpec((None, tq, wd), lambda bi, n: (bi, n, 0)),
                  pl.BlockSpec((None, WINDOW, wd), lambda bi, n: (bi, jnp.maximum(n * nblk - 1, 0), 0)),
                  pl.BlockSpec((SWA_KV, tq), lambda bi, n: (0, bi * steps + n)),
                  pl.BlockSpec((SWA_KV, WINDOW), lambda bi, n: (0, prev_col(bi, n))),
                  _const_spec((SWA_KV_HEADS, 2 * WINDOW, SWA_GROUP * WINDOW))],
        out_specs=pl.BlockSpec((SWA_Q, tq), lambda bi, n: (0, bi * steps + n)),
        out_shape=jax.ShapeDtypeStruct((SWA_Q, b * t), BF16),
        compiler_params=pltpu.CompilerParams(dimension_semantics=("arbitrary", "arbitrary")),
        name="swa_attention",
    )(sinks.astype(F32), sqt, sk3, sk3, svt, svt, bias)


def _split3(x):
    hi = x.astype(BF16).astype(F32)
    r = x - hi
    mid = r.astype(BF16).astype(F32)
    lo = (r - mid).astype(BF16).astype(F32)
    return [hi, mid, lo]


def _log_sigmoid(x):
    return jnp.minimum(x, 0.0) - jnp.log1p(jnp.exp(-jnp.abs(x)))


def _mlstm_kernel(qk_ref, vt_ref, sot_ref, gt_ref, ngb_ref, o_ref, state_ref, m_ref):
    L = MLSTM_CHUNK
    H, Dh = MLSTM_HEADS, MLSTM_HEAD_DIM
    SR = STATE_ROWS
    nc = qk_ref.shape[0] // L

    @pl.when(pl.program_id(1) == 0)
    def _():
        state_ref[...] = jnp.zeros(state_ref.shape, F32)
        m_ref[...] = jnp.zeros(m_ref.shape, F32)

    s_i = lax.broadcasted_iota(jnp.int32, (L, 2 * L), 0)
    t_i = lax.broadcasted_iota(jnp.int32, (L, 2 * L), 1) % L
    causal2 = s_i <= t_i
    tri_u = causal2[:, 0:L].astype(BF16)
    lo_half = lax.broadcasted_iota(jnp.int32, (L, LANES), 1) < Dh
    n_row = lax.broadcasted_iota(jnp.int32, (SR - Dh, L), 0) == 0
    block_sel = (lax.broadcasted_iota(jnp.int32, (6 * H, H * L), 0) % H
                 == lax.broadcasted_iota(jnp.int32, (6 * H, H * L), 1) // L)

    def stack(lo):
        return jnp.concatenate([gt_ref[lo:lo + H, c * L:(c + 1) * L] for c in range(nc)], axis=0)

    ig = stack(0)
    lf = _log_sigmoid(stack(H))
    a = sum(_dot(part.astype(BF16), tri_u) for part in _split3(lf))
    u = ig - a
    lane = lax.broadcasted_iota(jnp.int32, u.shape, 1)
    cm = u
    sh = 1
    while sh < L:
        cm = jnp.maximum(cm, jnp.where(lane >= sh, pltpu.roll(cm, sh, 1), NEG_INF))
        sh *= 2
    umax = cm[:, L - 1:L]
    g = a[:, L - 1:L]
    m = m_ref[:, 0:1]
    m_list = [m]
    for c in range(nc):
        gc = g[c * H:(c + 1) * H]
        m = jnp.maximum(gc + m, gc + umax[c * H:(c + 1) * H])
        m_list.append(m)
    m_ref[...] = jnp.broadcast_to(m, m_ref.shape)
    m_in = jnp.concatenate(m_list[:-1], axis=0)
    m_out = jnp.concatenate(m_list[1:], axis=0)
    big_m = jnp.maximum(cm, m_in)
    w = jnp.exp(u - umax)
    inter_w = jnp.exp(m_in - big_m)
    floor = jnp.exp(-(a + big_m))
    decay = jnp.exp(g + m_in - m_out)
    inject = jnp.exp(g + umax - m_out)
    u3 = _split3(u)
    nm3 = _split3(-big_m)
    ones_rows = jnp.ones((3 * H, L), F32)
    ones_blk = jnp.ones((3 * H, H * L), F32)

    def col(vec, r):
        return jnp.broadcast_to(vec[r:r + 1, :], (SR, 1))

    for c in range(nc):
        rows = slice(c * L, (c + 1) * L)
        g0 = c * H
        at = jnp.concatenate([part[g0:g0 + H] for part in u3] + [ones_rows], axis=0)
        nm = jnp.concatenate([part[g0:g0 + H] for part in nm3], axis=0)
        bb = jnp.concatenate([ones_blk, jnp.concatenate([nm] * H, axis=1)], axis=0)
        e = _dot_tn(at.astype(BF16), jnp.where(block_sel, bb, 0.0).astype(BF16))

        work = []
        for pair in range(H // 2):
            pr = slice(pair * LANES, (pair + 1) * LANES)
            qp = qk_ref[rows, pr]
            kp = qk_ref[rows, MLSTM_W + pair * LANES:MLSTM_W + (pair + 1) * LANES]
            qz2 = jnp.concatenate([jnp.where(lo_half, qp, 0), jnp.where(lo_half, 0, qp)], axis=0)
            vt2 = vt_ref[pr, rows]
            d2 = jnp.exp(jnp.where(causal2, e[:, pair * 2 * L:(pair + 1) * 2 * L], NEG_INF))
            p2 = _dot_nt(kp, qz2) * d2
            den2 = jnp.sum(p2, axis=0, keepdims=True)
            num2 = _dot(vt2, p2.astype(BF16))
            vt2f = vt2.astype(F32)
            aug = []
            for j in range(2):
                w_r = w[g0 + 2 * pair + j:g0 + 2 * pair + j + 1]
                aug += [vt2f[j * Dh:(j + 1) * Dh] * w_r, jnp.where(n_row, w_r, 0.0)]
            d_state = _dot(jnp.concatenate(aug, axis=0).astype(BF16), kp)
            work.append((qz2, den2, num2, d_state))

        for pair in range(H // 2):
            qz2, den2, num2, d_state = work[pair]
            state = state_ref[pair]
            inter2 = _dot_nt(state.astype(BF16), qz2)
            for j in range(2):
                h = 2 * pair + j
                r = g0 + h
                hr = slice(h * Dh, (h + 1) * Dh)
                tl = slice(j * L, (j + 1) * L)
                iw = inter_w[r:r + 1]
                num = num2[j * Dh:(j + 1) * Dh, tl] + iw * inter2[j * SR:j * SR + Dh, tl]
                den = den2[:, tl] + iw * inter2[j * SR + Dh:j * SR + Dh + 1, tl]
                hh = num * (1.0 / jnp.maximum(jnp.abs(den), floor[r:r + 1]))
                mu = jnp.mean(hh, axis=0, keepdims=True)
                ctr = hh - mu
                var = jnp.mean(ctr * ctr, axis=0, keepdims=True)
                y = sot_ref[hr, rows].astype(F32) * (ctr * lax.rsqrt(var + EPS) * ngb_ref[hr, :])
                o_ref[hr, rows] = y.astype(BF16)
            dec = jnp.concatenate([col(decay, g0 + 2 * pair), col(decay, g0 + 2 * pair + 1)], axis=0)
            inj = jnp.concatenate([col(inject, g0 + 2 * pair), col(inject, g0 + 2 * pair + 1)], axis=0)
            state_ref[pair] = dec * state + inj * d_state


def _mlstm(qk3, mvt, sot, gt, norm_g_b, ts=512):
    b, t, _ = qk3.shape
    steps = t // ts
    col = lambda rows: pl.BlockSpec((rows, ts), lambda bi, j: (0, bi * steps + j))
    return pl.pallas_call(
        _mlstm_kernel,
        grid=(b, steps),
        in_specs=[pl.BlockSpec((None, ts, 2 * MLSTM_W), lambda bi, j: (bi, j, 0)),
                  col(MLSTM_W), col(MLSTM_W), col(N_IF), _const_spec((MLSTM_W, LANES))],
        out_specs=col(MLSTM_W),
        out_shape=jax.ShapeDtypeStruct((MLSTM_W, b * t), BF16),
        scratch_shapes=[pltpu.VMEM((MLSTM_HEADS // 2, 2 * STATE_ROWS, LANES), F32),
                        pltpu.VMEM((MLSTM_HEADS, LANES), F32)],
        compiler_params=pltpu.CompilerParams(dimension_semantics=("arbitrary", "arbitrary")),
        name="mlstm",
    )(qk3, mvt, sot, gt, norm_g_b)


def _norm_matmul_kernel(x_ref, g_ref, w_ref, o_ref):
    o_ref[...] = _dot(_rms(x_ref[...], g_ref[...]).astype(BF16), w_ref[...]).astype(o_ref.dtype)


def _norm_matmul(x2, g, w, tm=512):
    n, d = x2.shape
    nout = w.shape[1]
    return pl.pallas_call(
        _norm_matmul_kernel,
        grid=(n // tm,),
        in_specs=[pl.BlockSpec((tm, d), lambda i: (i, 0)), _const_spec((1, d)), _const_spec((d, nout))],
        out_specs=pl.BlockSpec((tm, nout), lambda i: (i, 0)),
        out_shape=jax.ShapeDtypeStruct((n, nout), BF16),
        compiler_params=pltpu.CompilerParams(dimension_semantics=("arbitrary",)),
        name="mem_kv_proj",
    )(x2, g, w)


def _xattn_kernel(q_ref, kv_ref, o_ref):
    for h in range(X_HEADS):
        hc = slice(h * X_HEAD_DIM, (h + 1) * X_HEAD_DIM)
        s = _dot_nt(q_ref[:, hc], kv_ref[:, hc]) * X_HEAD_DIM ** -0.5
        p = jnp.exp(s - jnp.max(s, axis=-1, keepdims=True))
        denom = jnp.sum(p, axis=-1, keepdims=True)
        o = _dot(p.astype(BF16), kv_ref[:, X_W + h * X_HEAD_DIM:X_W + (h + 1) * X_HEAD_DIM]) / denom
        o_ref[:, hc] = o.astype(BF16)


def _xattn(xq3, kv3, tq=512):
    b, t, _ = xq3.shape
    m = kv3.shape[1]
    return pl.pallas_call(
        _xattn_kernel,
        grid=(b, t // tq),
        in_specs=[pl.BlockSpec((None, tq, X_W), lambda bi, j: (bi, j, 0)),
                  pl.BlockSpec((None, m, 2 * X_W), lambda bi, j: (bi, 0, 0))],
        out_specs=pl.BlockSpec((None, tq, X_W), lambda bi, j: (bi, j, 0)),
        out_shape=jax.ShapeDtypeStruct((b, t, X_W), BF16),
        compiler_params=pltpu.CompilerParams(dimension_semantics=("arbitrary", "arbitrary")),
        name="cross_attention",
    )(xq3, kv3)


def _post_kernel(x_ref, yst_ref, ymt_ref, yx_ref, gate_ref, wbs_ref, wbm_ref, wbx_ref, wout_ref,
                 gffn_ref, w1_ref, w2_ref, gfin_ref, o_ref, hid_ref, *, final_norm):
    def gate(k):
        return gate_ref[:, k * D_MODEL:(k + 1) * D_MODEL].astype(F32)

    merged = (gate(0) * _dot_tn(yst_ref[...], wbs_ref[...])
              + gate(1) * _dot_tn(ymt_ref[...], wbm_ref[...])
              + gate(2) * _dot(yx_ref[...], wbx_ref[...]))
    x1 = x_ref[...] + _dot(merged.astype(BF16), wout_ref[...])
    h = _rms(x1, gffn_ref[...]).astype(BF16)
    step = D_MODEL
    for c in range(D_FF // step):
        a = jnp.maximum(_dot(h, w1_ref[:, c * step:(c + 1) * step]), 0.0)
        hid_ref[:, c * step:(c + 1) * step] = (a * a).astype(BF16)
    x2 = x1 + _dot(hid_ref[...], w2_ref[...])
    if final_norm:
        x2 = _rms(x2, gfin_ref[...])
    o_ref[...] = x2


def _post(x2, yst, ymt, yx, gates, wbs, wbm, wbx, wout, gffn, w1, w2, gfin, final_norm, tm=256):
    n = x2.shape[0]
    row = lambda wd: pl.BlockSpec((tm, wd), lambda i: (i, 0))
    col = lambda rows: pl.BlockSpec((rows, tm), lambda i: (0, i))
    return pl.pallas_call(
        functools.partial(_post_kernel, final_norm=final_norm),
        grid=(n // tm,),
        in_specs=[row(D_MODEL), col(SWA_Q), col(MLSTM_W), row(X_W), row(N_BRANCH * D_MODEL),
                  _const_spec((SWA_Q, D_MODEL)), _const_spec((MLSTM_W, D_MODEL)), _const_spec((X_W, D_MODEL)),
                  _const_spec((D_MODEL, D_MODEL)), _const_spec((1, D_MODEL)),
                  _const_spec((D_MODEL, D_FF)), _const_spec((D_FF, D_MODEL)), _const_spec((1, D_MODEL))],
        out_specs=row(D_MODEL),
        out_shape=jax.ShapeDtypeStruct((n, D_MODEL), F32),
        scratch_shapes=[pltpu.VMEM((tm, D_FF), BF16)],
        compiler_params=pltpu.CompilerParams(
            dimension_semantics=("arbitrary",), vmem_limit_bytes=VMEM_LIMIT),
        name="merge_out_mlp",
    )(x2, yst, ymt, yx, gates, wbs, wbm, wbx, wout, gffn, w1, w2, gfin)


def _prep_in_weights(w_in_l, b_i_l, b_f_l):
    sizes = (SWA_Q, SWA_KV, SWA_KV, MLSTM_W, MLSTM_W, MLSTM_W, MLSTM_HEADS, MLSTM_HEADS, MLSTM_W, X_W,
             N_BRANCH * D_MODEL)
    offs = np.concatenate([[0], np.cumsum(sizes)])
    sq, sk, sv, mq, mk, mv, mi, mf, mo, xq, gates = (
        w_in_l[:, int(offs[j]):int(offs[j + 1])] for j in range(len(sizes)))
    w_tok = jnp.concatenate([mq, mk, sk, xq, gates], axis=1).astype(BF16)
    w_feat = jnp.concatenate([sq * HEAD_DIM ** -0.5, sv, mv, mo, mi, mf], axis=1).T.astype(BF16)
    b_if = jnp.concatenate([b_i_l, b_f_l]).astype(F32)[:, None]
    return w_tok, w_feat, b_if


def kernel(x, mem, rel_bias, g_mix, w_in, conv_w, b_i, b_f, mlstm_norm_g, sinks, g_mem, w_mem_kv,
           w_br_swa, w_br_mlstm, w_br_x, w_out, g_ffn, w_ff1, w_ff2, g_final):
    bsz, seq, d = x.shape
    mlen = mem.shape[1]
    depth = w_in.shape[0]
    n = bsz * seq
    bias = _bias_table(rel_bias)
    x2 = x.reshape(n, d).astype(F32)
    mem2 = mem.reshape(bsz * mlen, d).astype(F32)
    row = lambda v: v.astype(F32)[None, :]
    for l in range(depth):
        w_tok, w_feat, b_if = _prep_in_weights(w_in[l], b_i[l], b_f[l])
        mqk, sk, xq, gates, sqt, svt, mvt, sot, gt = _in_proj(
            x2, row(g_mix[l]), w_tok, w_feat, b_if, conv_w[l].astype(F32), seq)
        y_swa_t = _swa(sqt, sk.reshape(bsz, seq, -1), svt, sinks[l], bias)
        norm_g_b = jnp.broadcast_to(mlstm_norm_g[l].astype(F32)[:, None], (MLSTM_W, LANES))
        y_ml_t = _mlstm(mqk.reshape(bsz, seq, -1), mvt, sot, gt, norm_g_b)
        mem_kv = _norm_matmul(mem2, row(g_mem[l]), w_mem_kv[l].astype(BF16))
        y_x = _xattn(xq.reshape(bsz, seq, -1), mem_kv.reshape(bsz, mlen, -1))
        x2 = _post(x2, y_swa_t, y_ml_t, y_x.reshape(n, -1), gates,
                   w_br_swa[l].astype(BF16), w_br_mlstm[l].astype(BF16), w_br_x[l].astype(BF16),
                   w_out[l].astype(BF16), row(g_ffn[l]), w_ff1[l].astype(BF16), w_ff2[l].astype(BF16),
                   row(g_final), final_norm=(l == depth - 1))
    return x2.reshape(bsz, seq, d).astype(x.dtype)
```

```python
import functools
import math

import numpy as np
import jax
import jax.numpy as jnp
from jax import lax
from jax.experimental import pallas as pl
from jax.experimental.pallas import tpu as pltpu

D_MODEL = 1024
HEAD_DIM = 64
SWA_Q_HEADS = 8
SWA_KV_HEADS = 2
SWA_GROUP = SWA_Q_HEADS // SWA_KV_HEADS
WINDOW = 128
MLSTM_HEADS = 8
MLSTM_HEAD_DIM = 64
CONV_WIDTH = 4
X_HEADS = 4
X_HEAD_DIM = 128
N_BRANCH = 3
D_FF = 4 * D_MODEL
REL_BUCKETS = 32
REL_MAX_EXACT = 16
REL_MAX_DIST = 128
EPS = 1e-6
NEG_INF = -1e30

SWA_Q = SWA_Q_HEADS * HEAD_DIM
SWA_KV = SWA_KV_HEADS * HEAD_DIM
MLSTM_W = MLSTM_HEADS * MLSTM_HEAD_DIM
X_W = X_HEADS * X_HEAD_DIM
LANES = 128

N_IF = 2 * MLSTM_HEADS
R_SQ = 0
R_SV = R_SQ + SWA_Q
R_MV = R_SV + SWA_KV
R_MO = R_MV + MLSTM_W
R_IF = R_MO + MLSTM_W
R_END = R_IF + N_IF
LOG2E = math.log2(math.e)

MLSTM_CHUNK = 128
STATE_ROWS = MLSTM_HEAD_DIM + 16
CONV_HALO = 8

VMEM_LIMIT = 56 * 1024 * 1024

BF16 = jnp.bfloat16
F32 = jnp.float32


def _rms(x, g):
    return x * lax.rsqrt(jnp.mean(x * x, axis=-1, keepdims=True) + EPS) * g


def _const_spec(shape):
    nd = len(shape)
    return pl.BlockSpec(shape, lambda *_: (0,) * nd, pipeline_mode=pl.Buffered(1))


def _dot(a, b):
    return jnp.dot(a, b, preferred_element_type=F32)


def _dot_nt(a, b):
    return lax.dot_general(a, b, (((1,), (1,)), ((), ())), preferred_element_type=F32)


def _dot_tn(a, b):
    return lax.dot_general(a, b, (((0,), (0,)), ((), ())), preferred_element_type=F32)


def _in_proj_kernel(x_ref, g_ref, wmqk_ref, wskxq_ref, wgate_ref, wt_ref, bift_ref, convw_ref,
                    mqk_ref, sk_ref, xq_ref, gate_ref, sqt_ref, svt_ref, mvt_ref, sot_ref, gt_ref,
                    conv_buf, *, blocks_per_seq):
    i = pl.program_id(0)
    tm = x_ref.shape[0]

    @pl.when(i == 0)
    def _():
        conv_buf[tm:tm + CONV_HALO, :] = jnp.zeros((CONV_HALO, 2 * MLSTM_W), F32)

    h = _rms(x_ref[...], g_ref[...]).astype(BF16)

    prev_tail = conv_buf[tm:tm + CONV_HALO, :]
    conv_buf[0:CONV_HALO, :] = jnp.where(i % blocks_per_seq == 0, 0.0, prev_tail)
    conv_buf[CONV_HALO:CONV_HALO + tm, :] = _dot(h, wmqk_ref[...])
    for part, scale in ((0, 1.0), (1, MLSTM_HEAD_DIM ** -0.5)):
        cols = slice(part * MLSTM_W, (part + 1) * MLSTM_W)
        acc = None
        for j in range(CONV_WIDTH):
            r0 = CONV_HALO - (CONV_WIDTH - 1) + j
            term = convw_ref[j:j + 1, cols] * conv_buf[r0:r0 + tm, cols]
            acc = term if acc is None else acc + term
        act = acc * jax.nn.sigmoid(acc)
        mqk_ref[:, cols] = (act * scale).astype(BF16)

    feat = _dot_nt(wt_ref[...], h)
    sot_ref[...] = jax.nn.sigmoid(feat[R_MO:R_IF]).astype(BF16)
    gt_ref[...] = feat[R_IF:R_END] + bift_ref[...]
    sqt_ref[...] = feat[R_SQ:R_SV].astype(BF16)
    svt_ref[...] = feat[R_SV:R_MV].astype(BF16)
    mvt_ref[...] = feat[R_MV:R_MO].astype(BF16)
    for k in range(N_BRANCH):
        gate_ref[:, k * D_MODEL:(k + 1) * D_MODEL] = jax.nn.sigmoid(
            _dot(h, wgate_ref[:, k * D_MODEL:(k + 1) * D_MODEL])).astype(BF16)
    skxq = _dot(h, wskxq_ref[...])
    sk_ref[...] = skxq[:, 0:SWA_KV].astype(BF16)
    xq_ref[...] = skxq[:, SWA_KV:].astype(BF16)


def _in_proj(x2, g, wmqk, wskxq, wgate, wt, bift, convw, seq_len, tm=512):
    n = x2.shape[0]
    assert n % tm == 0 and seq_len % tm == 0
    row = lambda width: pl.BlockSpec((tm, width), lambda i: (i, 0))
    col = lambda rows: pl.BlockSpec((rows, tm), lambda i: (0, i))
    tok_widths = (2 * MLSTM_W, SWA_KV, X_W, N_BRANCH * D_MODEL)
    feat_rows = (SWA_Q, SWA_KV, MLSTM_W, MLSTM_W)
    out_shape = ([jax.ShapeDtypeStruct((n, wd), BF16) for wd in tok_widths]
                 + [jax.ShapeDtypeStruct((r, n), BF16) for r in feat_rows]
                 + [jax.ShapeDtypeStruct((N_IF, n), F32)])
    return pl.pallas_call(
        functools.partial(_in_proj_kernel, blocks_per_seq=seq_len // tm),
        grid=(n // tm,),
        in_specs=[row(D_MODEL), _const_spec((1, D_MODEL)), _const_spec(wmqk.shape),
                  _const_spec(wskxq.shape), _const_spec(wgate.shape),
                  _const_spec((R_END, D_MODEL)), _const_spec((N_IF, 1)),
                  _const_spec((CONV_WIDTH, 2 * MLSTM_W))],
        out_specs=[row(wd) for wd in tok_widths] + [col(r) for r in feat_rows] + [col(N_IF)],
        out_shape=out_shape,
        scratch_shapes=[pltpu.VMEM((tm + CONV_HALO, 2 * MLSTM_W), F32)],
        compiler_params=pltpu.CompilerParams(
            dimension_semantics=("arbitrary",), vmem_limit_bytes=VMEM_LIMIT),
        name="in_proj",
    )(x2, g, wmqk, wskxq, wgate, wt, bift, convw)


def _t5_bucket_table_t():
    qi = np.arange(WINDOW)[:, None]
    kj = np.arange(2 * WINDOW)[None, :]
    dist = qi + WINDOW - kj
    n = np.maximum(dist, 0)
    nf = np.maximum(n, 1).astype(np.float32)
    scale = np.float32((REL_BUCKETS - REL_MAX_EXACT) / math.log(REL_MAX_DIST / REL_MAX_EXACT))
    large = REL_MAX_EXACT + (np.log(nf / np.float32(REL_MAX_EXACT)) * scale).astype(np.int32)
    large = np.minimum(large, REL_BUCKETS - 1)
    bucket = np.where(n < REL_MAX_EXACT, n, large)
    band = (dist >= 0) & (dist < WINDOW)
    return np.where(band, bucket, -1).astype(np.int32).T.copy()


def _bias_kernel(rel_ref, bucket_ref, o_ref):
    bucket = bucket_ref[...]
    for h in range(SWA_Q_HEADS):
        acc = jnp.full(bucket.shape, NEG_INF, F32)
        for b in range(REL_BUCKETS):
            acc = jnp.where(bucket == b, rel_ref[b, h] * LOG2E, acc)
        hk, g = divmod(h, SWA_GROUP)
        o_ref[hk, :, g * WINDOW:(g + 1) * WINDOW] = acc


def _bias_table(rel_bias):
    bucket = jnp.asarray(_t5_bucket_table_t())
    shape = (SWA_KV_HEADS, 2 * WINDOW, SWA_GROUP * WINDOW)
    return pl.pallas_call(
        _bias_kernel,
        in_specs=[pl.BlockSpec(memory_space=pltpu.SMEM),
                  pl.BlockSpec((2 * WINDOW, WINDOW), lambda: (0, 0))],
        out_specs=pl.BlockSpec(shape, lambda: (0, 0, 0)),
        out_shape=jax.ShapeDtypeStruct(shape, F32),
        name="t5_bias_table",
    )(rel_bias.astype(F32), bucket)


def _swa_kernel(sinks_ref, qt_ref, k_ref, kprev_ref, vt_ref, vtprev_ref, bias_ref, o_ref):
    n = pl.program_id(1)
    tq = k_ref.shape[0]
    cols = SWA_GROUP * WINDOW
    kj = lax.broadcasted_iota(jnp.int32, (2 * WINDOW, cols), 0)
    zeros = jnp.zeros((HEAD_DIM, cols), BF16)
    units = [(blk, hk) for blk in range(tq // WINDOW) for hk in range(SWA_KV_HEADS)]

    def logits(blk, hk):
        r0 = blk * WINDOW
        if blk == 0:
            k2 = jnp.concatenate([kprev_ref[...], k_ref[0:WINDOW, :]], axis=0)
        else:
            k2 = k_ref[r0 - WINDOW:r0 + WINDOW, :]
        heads = [hk * SWA_GROUP + g for g in range(SWA_GROUP)]
        q4t = jnp.concatenate(
            [qt_ref[hd * HEAD_DIM:(hd + 1) * HEAD_DIM, r0:r0 + WINDOW] for hd in heads], axis=1)
        rhs = jnp.concatenate([q4t, zeros] if hk == 0 else [zeros, q4t], axis=0)
        return _dot(k2, rhs)

    def attend(blk, hk, qk):
        r0 = blk * WINDOW
        vr = slice(hk * HEAD_DIM, (hk + 1) * HEAD_DIM)
        if blk == 0:
            v2t = jnp.concatenate([vtprev_ref[vr, :], vt_ref[vr, 0:WINDOW]], axis=1)
        else:
            v2t = vt_ref[vr, r0 - WINDOW:r0 + WINDOW]
        heads = [hk * SWA_GROUP + g for g in range(SWA_GROUP)]
        s = qk * LOG2E + bias_ref[hk]
        if blk == 0:
            s = jnp.where((kj >= WINDOW) | (n > 0), s, NEG_INF)
        sink = jnp.concatenate(
            [jnp.full((1, WINDOW), sinks_ref[hd] * LOG2E, F32) for hd in heads], axis=1)
        m = jnp.maximum(jnp.max(s, axis=0, keepdims=True), sink)
        p = jnp.exp2(s - m)
        denom = jnp.sum(p, axis=0, keepdims=True) + jnp.exp2(sink - m)
        o = _dot(v2t, p.astype(BF16)) * (1.0 / denom)
        for g, hd in enumerate(heads):
            o_ref[hd * HEAD_DIM:(hd + 1) * HEAD_DIM, r0:r0 + WINDOW] = (
                o[:, g * WINDOW:(g + 1) * WINDOW].astype(BF16))

    lookahead = 2
    pending = {i: logits(*units[i]) for i in range(min(lookahead, len(units)))}
    for i, unit in enumerate(units):
        attend(*unit, pending.pop(i))
        if i + lookahead < len(units):
            pending[i + lookahead] = logits(*units[i + lookahead])


def _swa(sqt, sk3, svt, sinks, bias, tq=512):
    b, t, wd = sk3.shape
    nblk = tq // WINDOW
    steps = t // tq
    prev_col = lambda bi, n: jnp.maximum((bi * steps + n) * nblk - 1, 0)
    return pl.pallas_call(
        _swa_kernel,
        grid=(b, steps),
        in_specs=[pl.BlockSpec(memory_space=pltpu.SMEM),
                  pl.BlockSpec((SWA_Q, tq), lambda bi, n: (0, bi * steps + n)),
                  pl.BlockSpec((None, tq, wd), lambda bi, n: (bi, n, 0)),
                  pl.BlockSpec((None, WINDOW, wd), lambda bi, n: (bi, jnp.maximum(n * nblk - 1, 0), 0)),
                  pl.BlockSpec((SWA_KV, tq), lambda bi, n: (0, bi * steps + n)),
                  pl.BlockSpec((SWA_KV, WINDOW), lambda bi, n: (0, prev_col(bi, n))),
                  _const_spec((SWA_KV_HEADS, 2 * WINDOW, SWA_GROUP * WINDOW))],
        out_specs=pl.BlockSpec((SWA_Q, tq), lambda bi, n: (0, bi * steps + n)),
        out_shape=jax.ShapeDtypeStruct((SWA_Q, b * t), BF16),
        compiler_params=pltpu.CompilerParams(dimension_semantics=("arbitrary", "arbitrary")),
        name="swa_attention",
    )(sinks.astype(F32), sqt, sk3, sk3, svt, svt, bias)


def _split3(x):
    hi = x.astype(BF16).astype(F32)
    r = x - hi
    mid = r.astype(BF16).astype(F32)
    lo = (r - mid).astype(BF16).astype(F32)
    return [hi, mid, lo]


def _log_sigmoid(x):
    return jnp.minimum(x, 0.0) - jnp.log1p(jnp.exp(-jnp.abs(x)))


def _mlstm_kernel(qk_ref, vt_ref, sot_ref, gt_ref, ngb_ref, o_ref, state_ref, m_ref):
    L = MLSTM_CHUNK
    H, Dh = MLSTM_HEADS, MLSTM_HEAD_DIM
    SR = STATE_ROWS
    nc = qk_ref.shape[0] // L

    @pl.when(pl.program_id(1) == 0)
    def _():
        state_ref[...] = jnp.zeros(state_ref.shape, F32)
        m_ref[...] = jnp.zeros(m_ref.shape, F32)

    s_i = lax.broadcasted_iota(jnp.int32, (L, 2 * L), 0)
    t_i = lax.broadcasted_iota(jnp.int32, (L, 2 * L), 1) % L
    causal2 = s_i <= t_i
    tri_u = causal2[:, 0:L].astype(BF16)
    lo_half = lax.broadcasted_iota(jnp.int32, (L, LANES), 1) < Dh
    n_row = lax.broadcasted_iota(jnp.int32, (SR - Dh, L), 0) == 0
    block_sel = (lax.broadcasted_iota(jnp.int32, (6 * H, H * L), 0) % H
                 == lax.broadcasted_iota(jnp.int32, (6 * H, H * L), 1) // L)

    def stack(lo):
        return jnp.concatenate([gt_ref[lo:lo + H, c * L:(c + 1) * L] for c in range(nc)], axis=0)

    ig = stack(0)
    lf = _log_sigmoid(stack(H))
    a = sum(_dot(part.astype(BF16), tri_u) for part in _split3(lf))

    lookahead = 2

    def stage_q(c):
        rows = slice(c * L, (c + 1) * L)
        pairs = []
        for pair in range(H // 2):
            pr = slice(pair * LANES, (pair + 1) * LANES)
            qp = qk_ref[rows, pr]
            kp = qk_ref[rows, MLSTM_W + pair * LANES:MLSTM_W + (pair + 1) * LANES]
            qz2 = jnp.concatenate([jnp.where(lo_half, qp, 0), jnp.where(lo_half, 0, qp)], axis=0)
            pairs.append(dict(qz2=qz2, kp=kp, s2=_dot_nt(kp, qz2)))
        return pairs

    staged_q = {c: stage_q(c) for c in range(min(lookahead, nc))}
    u = ig - a
    lane = lax.broadcasted_iota(jnp.int32, u.shape, 1)
    cm = u
    sh = 1
    while sh < L:
        cm = jnp.maximum(cm, jnp.where(lane >= sh, pltpu.roll(cm, sh, 1), NEG_INF))
        sh *= 2
    umax = cm[:, L - 1:L]
    g = a[:, L - 1:L]
    m = m_ref[:, 0:1]
    m_list = [m]
    for c in range(nc):
        gc = g[c * H:(c + 1) * H]
        m = jnp.maximum(gc + m, gc + umax[c * H:(c + 1) * H])
        m_list.append(m)
    m_ref[...] = jnp.broadcast_to(m, m_ref.shape)
    m_in = jnp.concatenate(m_list[:-1], axis=0)
    m_out = jnp.concatenate(m_list[1:], axis=0)
    big_m = jnp.maximum(cm, m_in)
    w = jnp.exp(u - umax)
    inter_w = jnp.exp(m_in - big_m)
    floor = jnp.exp(-(a + big_m))
    decay = jnp.exp(g + m_in - m_out)
    inject = jnp.exp(g + umax - m_out)
    u3 = _split3(u)
    nm3 = _split3(-big_m)
    ones_rows = jnp.ones((3 * H, L), F32)
    ones_blk = jnp.ones((3 * H, H * L), F32)

    def col(vec, r):
        return jnp.broadcast_to(vec[r:r + 1, :], (SR, 1))

    def stage_a(c, pairs):
        rows = slice(c * L, (c + 1) * L)
        g0 = c * H
        at = jnp.concatenate([part[g0:g0 + H] for part in u3] + [ones_rows], axis=0)
        nm = jnp.concatenate([part[g0:g0 + H] for part in nm3], axis=0)
        bb = jnp.concatenate([ones_blk, jnp.concatenate([nm] * H, axis=1)], axis=0)
        e = _dot_tn(at.astype(BF16), jnp.where(block_sel, bb, 0.0).astype(BF16))
        for pair, pd in enumerate(pairs):
            vt2 = vt_ref[pair * LANES:(pair + 1) * LANES, rows]
            vt2f = vt2.astype(F32)
            aug = []
            for j in range(2):
                w_r = w[g0 + 2 * pair + j:g0 + 2 * pair + j + 1]
                aug += [vt2f[j * Dh:(j + 1) * Dh] * w_r, jnp.where(n_row, w_r, 0.0)]
            pd["d_state"] = _dot(jnp.concatenate(aug, axis=0).astype(BF16), pd["kp"])
            pd["vt2"] = vt2
        return e, pairs

    def stage_b(c, e, pairs):
        for pair, pd in enumerate(pairs):
            d2 = jnp.exp(jnp.where(causal2, e[:, pair * 2 * L:(pair + 1) * 2 * L], NEG_INF))
            p2 = pd["s2"] * d2
            pd["den2"] = jnp.sum(p2, axis=0, keepdims=True)
            pd["num2"] = _dot(pd["vt2"], p2.astype(BF16))

    def stage_c(c, pairs, states):
        rows = slice(c * L, (c + 1) * L)
        g0 = c * H
        for pair, pd in enumerate(pairs):
            state = states[pair]
            inter2 = _dot_nt(state.astype(BF16), pd["qz2"])
            for j in range(2):
                h = 2 * pair + j
                r = g0 + h
                hr = slice(h * Dh, (h + 1) * Dh)
                tl = slice(j * L, (j + 1) * L)
                iw = inter_w[r:r + 1]
                num = pd["num2"][j * Dh:(j + 1) * Dh, tl] + iw * inter2[j * SR:j * SR + Dh, tl]
                den = pd["den2"][:, tl] + iw * inter2[j * SR + Dh:j * SR + Dh + 1, tl]
                hh = num * (1.0 / jnp.maximum(jnp.abs(den), floor[r:r + 1]))
                mu = jnp.mean(hh, axis=0, keepdims=True)
                ctr = hh - mu
                var = jnp.mean(ctr * ctr, axis=0, keepdims=True)
                y = sot_ref[hr, rows].astype(F32) * (ctr * lax.rsqrt(var + EPS) * ngb_ref[hr, :])
                o_ref[hr, rows] = y.astype(BF16)
            dec = jnp.concatenate([col(decay, g0 + 2 * pair), col(decay, g0 + 2 * pair + 1)], axis=0)
            inj = jnp.concatenate([col(inject, g0 + 2 * pair), col(inject, g0 + 2 * pair + 1)], axis=0)
            states[pair] = dec * state + inj * pd["d_state"]

    states = [state_ref[pair] for pair in range(H // 2)]
    staged = {c: stage_a(c, staged_q.pop(c)) for c in range(min(lookahead, nc))}
    for c in range(nc):
        e, pairs = staged.pop(c)
        stage_b(c, e, pairs)
        stage_c(c, pairs, states)
        if c + lookahead < nc:
            staged[c + lookahead] = stage_a(c + lookahead, stage_q(c + lookahead))
    for pair in range(H // 2):
        state_ref[pair] = states[pair]


def _mlstm(qk3, mvt, sot, gt, norm_g_b, ts=2048):
    b, t, _ = qk3.shape
    steps = t // ts
    col = lambda rows: pl.BlockSpec((rows, ts), lambda bi, j: (0, bi * steps + j))
    return pl.pallas_call(
        _mlstm_kernel,
        grid=(b, steps),
        in_specs=[pl.BlockSpec((None, ts, 2 * MLSTM_W), lambda bi, j: (bi, j, 0)),
                  col(MLSTM_W), col(MLSTM_W), col(N_IF), _const_spec((MLSTM_W, LANES))],
        out_specs=col(MLSTM_W),
        out_shape=jax.ShapeDtypeStruct((MLSTM_W, b * t), BF16),
        scratch_shapes=[pltpu.VMEM((MLSTM_HEADS // 2, 2 * STATE_ROWS, LANES), F32),
                        pltpu.VMEM((MLSTM_HEADS, LANES), F32)],
        compiler_params=pltpu.CompilerParams(
            dimension_semantics=("arbitrary", "arbitrary"), vmem_limit_bytes=VMEM_LIMIT),
        name="mlstm",
    )(qk3, mvt, sot, gt, norm_g_b)


def _norm_matmul_kernel(x_ref, g_ref, w_ref, o_ref):
    o_ref[...] = _dot(_rms(x_ref[...], g_ref[...]).astype(BF16), w_ref[...]).astype(o_ref.dtype)


def _norm_matmul(x2, g, w, tm=512):
    n, d = x2.shape
    nout = w.shape[1]
    return pl.pallas_call(
        _norm_matmul_kernel,
        grid=(n // tm,),
        in_specs=[pl.BlockSpec((tm, d), lambda i: (i, 0)), _const_spec((1, d)), _const_spec((d, nout))],
        out_specs=pl.BlockSpec((tm, nout), lambda i: (i, 0)),
        out_shape=jax.ShapeDtypeStruct((n, nout), BF16),
        compiler_params=pltpu.CompilerParams(dimension_semantics=("arbitrary",)),
        name="mem_kv_proj",
    )(x2, g, w)


def _xattn_kernel(q_ref, kv_ref, o_ref):
    cols = [slice(h * X_HEAD_DIM, (h + 1) * X_HEAD_DIM) for h in range(X_HEADS)]
    logits = [_dot_nt(q_ref[:, hc], kv_ref[:, hc]) for hc in cols]
    for h, hc in enumerate(cols):
        s = logits[h] * X_HEAD_DIM ** -0.5
        p = jnp.exp(s - jnp.max(s, axis=-1, keepdims=True))
        denom = jnp.sum(p, axis=-1, keepdims=True)
        o = _dot(p.astype(BF16), kv_ref[:, X_W + h * X_HEAD_DIM:X_W + (h + 1) * X_HEAD_DIM]) / denom
        o_ref[:, hc] = o.astype(BF16)


def _xattn(xq3, kv3, tq=512):
    b, t, _ = xq3.shape
    m = kv3.shape[1]
    return pl.pallas_call(
        _xattn_kernel,
        grid=(b, t // tq),
        in_specs=[pl.BlockSpec((None, tq, X_W), lambda bi, j: (bi, j, 0)),
                  pl.BlockSpec((None, m, 2 * X_W), lambda bi, j: (bi, 0, 0))],
        out_specs=pl.BlockSpec((None, tq, X_W), lambda bi, j: (bi, j, 0)),
        out_shape=jax.ShapeDtypeStruct((b, t, X_W), BF16),
        compiler_params=pltpu.CompilerParams(dimension_semantics=("arbitrary", "arbitrary")),
        name="cross_attention",
    )(xq3, kv3)


def _post_kernel(x_ref, yst_ref, ymt_ref, yx_ref, gate_ref, wbs_ref, wbm_ref, wbx_ref, wout_ref,
                 gffn_ref, w1_ref, w2_ref, gfin_ref, o_ref, hid_ref, *, final_norm):
    tm = x_ref.shape[0]
    halves = [slice(0, tm // 2), slice(tm // 2, tm)]

    def gate(rs, k):
        return gate_ref[rs, k * D_MODEL:(k + 1) * D_MODEL].astype(F32)

    branches = [(_dot_tn(yst_ref[:, rs], wbs_ref[...]), _dot_tn(ymt_ref[:, rs], wbm_ref[...]),
                 _dot(yx_ref[rs, :], wbx_ref[...])) for rs in halves]
    x1 = []
    for rs, (bs, bm, bx) in zip(halves, branches):
        merged = gate(rs, 0) * bs + gate(rs, 1) * bm + gate(rs, 2) * bx
        x1.append(x_ref[rs, :] + _dot(merged.astype(BF16), wout_ref[...]))
    hs = [_rms(x, gffn_ref[...]).astype(BF16) for x in x1]
    step = D_MODEL
    for c in range(D_FF // step):
        for rs, h in zip(halves, hs):
            a = jnp.maximum(_dot(h, w1_ref[:, c * step:(c + 1) * step]), 0.0)
            hid_ref[rs, c * step:(c + 1) * step] = (a * a).astype(BF16)
    for rs, x in zip(halves, x1):
        x2 = x + _dot(hid_ref[rs, :], w2_ref[...])
        if final_norm:
            x2 = _rms(x2, gfin_ref[...])
        o_ref[rs, :] = x2


def _post(x2, yst, ymt, yx, gates, wbs, wbm, wbx, wout, gffn, w1, w2, gfin, final_norm, tm=512):
    n = x2.shape[0]
    row = lambda wd: pl.BlockSpec((tm, wd), lambda i: (i, 0))
    col = lambda rows: pl.BlockSpec((rows, tm), lambda i: (0, i))
    return pl.pallas_call(
        functools.partial(_post_kernel, final_norm=final_norm),
        grid=(n // tm,),
        in_specs=[row(D_MODEL), col(SWA_Q), col(MLSTM_W), row(X_W), row(N_BRANCH * D_MODEL),
                  _const_spec((SWA_Q, D_MODEL)), _const_spec((MLSTM_W, D_MODEL)), _const_spec((X_W, D_MODEL)),
                  _const_spec((D_MODEL, D_MODEL)), _const_spec((1, D_MODEL)),
                  _const_spec((D_MODEL, D_FF)), _const_spec((D_FF, D_MODEL)), _const_spec((1, D_MODEL))],
        out_specs=row(D_MODEL),
        out_shape=jax.ShapeDtypeStruct((n, D_MODEL), F32),
        scratch_shapes=[pltpu.VMEM((tm, D_FF), BF16)],
        compiler_params=pltpu.CompilerParams(
            dimension_semantics=("arbitrary",), vmem_limit_bytes=VMEM_LIMIT),
        name="merge_out_mlp",
    )(x2, yst, ymt, yx, gates, wbs, wbm, wbx, wout, gffn, w1, w2, gfin)


def _prep_in_weights(w_in_l, b_i_l, b_f_l):
    sizes = (SWA_Q, SWA_KV, SWA_KV, MLSTM_W, MLSTM_W, MLSTM_W, MLSTM_HEADS, MLSTM_HEADS, MLSTM_W, X_W,
             N_BRANCH * D_MODEL)
    offs = np.concatenate([[0], np.cumsum(sizes)])
    sq, sk, sv, mq, mk, mv, mi, mf, mo, xq, gates = (
        w_in_l[:, int(offs[j]):int(offs[j + 1])] for j in range(len(sizes)))
    w_mqk = jnp.concatenate([mq, mk], axis=1).astype(BF16)
    w_skxq = jnp.concatenate([sk, xq], axis=1).astype(BF16)
    w_gate = gates.astype(BF16)
    w_feat = jnp.concatenate([sq * HEAD_DIM ** -0.5, sv, mv, mo, mi, mf], axis=1).T.astype(BF16)
    b_if = jnp.concatenate([b_i_l, b_f_l]).astype(F32)[:, None]
    return w_mqk, w_skxq, w_gate, w_feat, b_if


def kernel(x, mem, rel_bias, g_mix, w_in, conv_w, b_i, b_f, mlstm_norm_g, sinks, g_mem, w_mem_kv,
           w_br_swa, w_br_mlstm, w_br_x, w_out, g_ffn, w_ff1, w_ff2, g_final):
    bsz, seq, d = x.shape
    mlen = mem.shape[1]
    depth = w_in.shape[0]
    n = bsz * seq
    bias = _bias_table(rel_bias)
    x2 = x.reshape(n, d).astype(F32)
    mem2 = mem.reshape(bsz * mlen, d).astype(F32)
    row = lambda v: v.astype(F32)[None, :]
    for l in range(depth):
        w_mqk, w_skxq, w_gate, w_feat, b_if = _prep_in_weights(w_in[l], b_i[l], b_f[l])
        mqk, sk, xq, gates, sqt, svt, mvt, sot, gt = _in_proj(
            x2, row(g_mix[l]), w_mqk, w_skxq, w_gate, w_feat, b_if, conv_w[l].astype(F32), seq)
        y_swa_t = _swa(sqt, sk.reshape(bsz, seq, -1), svt, sinks[l], bias)
        norm_g_b = jnp.broadcast_to(mlstm_norm_g[l].astype(F32)[:, None], (MLSTM_W, LANES))
        y_ml_t = _mlstm(mqk.reshape(bsz, seq, -1), mvt, sot, gt, norm_g_b)
        mem_kv = _norm_matmul(mem2, row(g_mem[l]), w_mem_kv[l].astype(BF16))
        y_x = _xattn(xq.reshape(bsz, seq, -1), mem_kv.reshape(bsz, mlen, -1))
        x2 = _post(x2, y_swa_t, y_ml_t, y_x.reshape(n, -1), gates,
                   w_br_swa[l].astype(BF16), w_br_mlstm[l].astype(BF16), w_br_x[l].astype(BF16),
                   w_out[l].astype(BF16), row(g_ffn[l]), w_ff1[l].astype(BF16), w_ff2[l].astype(BF16),
                   row(g_final), final_norm=(l == depth - 1))
    return x2.reshape(bsz, seq, d).astype(x.dtype)
```
